```python
import jax, jax.numpy as jnp
from jax import lax
import numpy as np

D_MODEL = 1024
BATCH = 8
SEQ = 4096
DEPTH = 2

GRID_W = 64
CTX_LEN = 256
N_MIXERS = 4
MIX_WIDTH = D_MODEL
GROUP_WIDTH = MIX_WIDTH // N_MIXERS
HEAD_DIM = 64
NA_HEADS = GROUP_WIDTH // HEAD_DIM
FOURIER_HEADS = GROUP_WIDTH // HEAD_DIM
CONV_K = 3
POOL_WINDOWS = (2, 4, 8, 16)
POOL_GROUPS = len(POOL_WINDOWS)
POOL_GROUP = GROUP_WIDTH // POOL_GROUPS
WIN_ROWS = 8
WIN_COLS = 16
N_PROJ_CHUNKS = 8
IN_COLS = N_PROJ_CHUNKS * GROUP_WIDTH
N_EXPERTS = 16
N_EXPERT_GROUPS = 4
EXPERTS_PER_GROUP = N_EXPERTS // N_EXPERT_GROUPS
TOP_K = 2
MOE_HIDDEN = D_MODEL // 2
EPS = 1e-6

kernel_name = "hybrid_parallel_mixer_dit_block"

F32 = jnp.float32


def rms_norm(x, g):
    xf = x.astype(F32)
    y = xf * lax.rsqrt(jnp.mean(jnp.square(xf), axis=-1, keepdims=True) + EPS)
    return (y * g.astype(F32)).astype(x.dtype)


def gated_short_conv(val, gate_b, gate_c, w):
    n = val.shape[1]
    pad = CONV_K // 2
    z = jnp.pad(gate_c * val, ((0, 0), (pad, CONV_K - 1 - pad), (0, 0)))
    conv = sum(z[:, j:j + n] * w[j] for j in range(CONV_K))
    return gate_b * conv


def multiscale_pool(u, pool_w, pool_scale):
    b, n, _ = u.shape
    uf = u.astype(F32)
    csum = jnp.concatenate([jnp.zeros((b, 1, GROUP_WIDTH), F32), jnp.cumsum(uf, axis=1)], axis=1)
    t = jnp.arange(n)
    outs = []
    for gi, w in enumerate(POOL_WINDOWS):
        lo = jnp.maximum(t - w // 2, 0)
        hi = jnp.minimum(t + w // 2, n)
        sl = slice(gi * POOL_GROUP, (gi + 1) * POOL_GROUP)
        win_sum = csum[:, hi, sl] - csum[:, lo, sl]
        cnt = (hi - lo).astype(F32)[None, :, None]
        outs.append(win_sum / cnt - uf[:, :, sl])
    d = jnp.stack(outs, axis=2)
    y = jnp.einsum('bngc,gcd->bngd', d, pool_w.astype(F32))
    return (y.reshape(b, n, GROUP_WIDTH) * pool_scale.astype(F32)).astype(u.dtype)


def fourier_mix(u):
    b, n, _ = u.shape
    uh = u.astype(F32).reshape(b, n, FOURIER_HEADS, HEAD_DIM)
    y = jnp.fft.fft2(uh, axes=(1, 3), norm='ortho').real
    return y.reshape(b, n, GROUP_WIDTH).astype(u.dtype)


def neighborhood_attention(q, k, v, kc, vc, rpb):
    b, n, _ = q.shape
    rows = n // GRID_W
    kr = min(WIN_ROWS, rows)
    scale = HEAD_DIM ** -0.5

    def to_grid(a):
        return a.reshape(b, rows, GRID_W, NA_HEADS, HEAD_DIM).transpose(0, 3, 1, 2, 4)

    qg, kg, vg = to_grid(q), to_grid(k), to_grid(v)
    kch = kc.reshape(b, -1, NA_HEADS, HEAD_DIM).transpose(0, 2, 1, 3)
    vch = vc.reshape(b, -1, NA_HEADS, HEAD_DIM).transpose(0, 2, 1, 3)
    r = jnp.arange(rows)
    col = jnp.arange(GRID_W)
    row_idx = jnp.clip(r - kr // 2, 0, rows - kr)[:, None] + jnp.arange(kr)[None, :]
    col_idx = jnp.clip(col - WIN_COLS // 2, 0, GRID_W - WIN_COLS)[:, None] + jnp.arange(WIN_COLS)[None, :]
    k_band = kg[:, :, row_idx]
    v_band = vg[:, :, row_idx]
    s_band = jnp.einsum('bhrqd,bhrkwd->bhrqkw', qg, k_band, preferred_element_type=F32) * scale
    sel = jax.nn.one_hot(col_idx, GRID_W, dtype=F32)
    s_win = jnp.einsum('bhrqkw,qjw->bhrqkj', s_band, sel)
    rel_r = row_idx - r[:, None] + (WIN_ROWS - 1)
    rel_c = col_idx - col[:, None] + (WIN_COLS - 1)
    bias = rpb[:, rel_r[:, None, :, None], rel_c[None, :, None, :]]
    n_win = kr * WIN_COLS
    s_win = (s_win + bias[None].astype(F32)).reshape(b, NA_HEADS, rows, GRID_W, n_win)
    s_ctx = jnp.einsum('bhrqd,bhld->bhrql', qg, kch, preferred_element_type=F32) * scale
    p = jax.nn.softmax(jnp.concatenate([s_win, s_ctx], axis=-1), axis=-1)
    p_win = p[..., :n_win].reshape(b, NA_HEADS, rows, GRID_W, kr, WIN_COLS)
    p_ctx = p[..., n_win:]
    p_band = jnp.einsum('bhrqkj,qjw->bhrqkw', p_win, sel)
    o = (jnp.einsum('bhrqkw,bhrkwd->bhrqd', p_band.astype(v.dtype), v_band)
         + jnp.einsum('bhrql,bhld->bhrqd', p_ctx.astype(v.dtype), vch))
    return o.transpose(0, 2, 3, 1, 4).reshape(b, n, GROUP_WIDTH)


def context_attention(qc, kc, vc):
    b, l, _ = qc.shape
    sp = lambda a: a.reshape(b, l, NA_HEADS, HEAD_DIM)
    s = jnp.einsum('blhd,bmhd->bhlm', sp(qc), sp(kc), preferred_element_type=F32) * (HEAD_DIM ** -0.5)
    p = jax.nn.softmax(s, axis=-1)
    o = jnp.einsum('bhlm,bmhd->blhd', p.astype(vc.dtype), sp(vc))
    return o.reshape(b, l, GROUP_WIDTH)


def moe_ffn(h, router_w, router_b, w1, w3, w2):
    scores = jax.nn.sigmoid(jnp.dot(h.astype(F32), router_w.astype(F32)))
    biased = scores + router_b.astype(F32)
    grp = biased.reshape(h.shape[0], N_EXPERT_GROUPS, EXPERTS_PER_GROUP)
    grp_score = jnp.sum(lax.top_k(grp, TOP_K)[0], axis=-1)
    best = jnp.argmax(grp_score, axis=-1)
    expert_group = jnp.arange(N_EXPERTS) // EXPERTS_PER_GROUP
    masked = jnp.where(expert_group[None, :] == best[:, None], biased, -jnp.inf)
    _, idx = lax.top_k(masked, TOP_K)
    wsel = jnp.take_along_axis(scores, idx, axis=-1)
    wsel = wsel / jnp.sum(wsel, axis=-1, keepdims=True)
    gates = jnp.sum(jax.nn.one_hot(idx, N_EXPERTS, dtype=F32) * wsel[..., None], axis=1)
    y = jnp.zeros(h.shape, F32)
    for e in range(N_EXPERTS):
        he = jax.nn.silu(h @ w1[e]) * (h @ w3[e])
        y = y + gates[:, e:e + 1] * (he @ w2[e]).astype(F32)
    return y.astype(h.dtype)


def setup_inputs(seed: int = 0) -> dict:
    key = jax.random.key(seed)
    ks = jax.random.split(key, 22)
    nrm = lambda k, shape, s: jax.random.normal(k, shape, F32) * s
    return {
        "x": nrm(ks[0], (BATCH, SEQ, D_MODEL), 1.0),
        "c": nrm(ks[1], (BATCH, D_MODEL), 1.0),
        "ctx": nrm(ks[2], (BATCH, CTX_LEN, D_MODEL), 1.0),
        "c_ctx": nrm(ks[3], (D_MODEL,), 1.0),
        "ada_w": nrm(ks[4], (DEPTH, D_MODEL, 6 * D_MODEL), 0.5 * D_MODEL ** -0.5),
        "ada_b": nrm(ks[5], (DEPTH, 6 * D_MODEL), 0.02),
        "norm1_g": 1.0 + nrm(ks[6], (DEPTH, D_MODEL), 0.02),
        "norm2_g": 1.0 + nrm(ks[7], (DEPTH, D_MODEL), 0.02),
        "w_in": nrm(ks[8], (DEPTH, D_MODEL, IN_COLS), D_MODEL ** -0.5),
        "conv_w": nrm(ks[9], (DEPTH, CONV_K, GROUP_WIDTH), CONV_K ** -0.5),
        "pool_w": nrm(ks[10], (DEPTH, POOL_GROUPS, POOL_GROUP, POOL_GROUP), POOL_GROUP ** -0.5),
        "pool_scale": 1.0 + nrm(ks[11], (DEPTH, GROUP_WIDTH), 0.1),
        "na_rpb": nrm(ks[12], (DEPTH, NA_HEADS, 2 * WIN_ROWS - 1, 2 * WIN_COLS - 1), 0.1),
        "w_out": nrm(ks[13], (DEPTH, MIX_WIDTH, D_MODEL), MIX_WIDTH ** -0.5),
        "router_w": nrm(ks[14], (D_MODEL, N_EXPERTS), D_MODEL ** -0.5),
        "router_b": nrm(ks[15], (N_EXPERTS,), 0.01),
        "moe_w1": nrm(ks[16], (DEPTH, N_EXPERTS, D_MODEL, MOE_HIDDEN), D_MODEL ** -0.5),
        "moe_w3": nrm(ks[17], (DEPTH, N_EXPERTS, D_MODEL, MOE_HIDDEN), D_MODEL ** -0.5),
        "moe_w2": nrm(ks[18], (DEPTH, N_EXPERTS, MOE_HIDDEN, D_MODEL), MOE_HIDDEN ** -0.5),
        "final_g": 1.0 + nrm(ks[19], (D_MODEL,), 0.02),
    }


def reference(x, c, ctx, c_ctx, ada_w, ada_b, norm1_g, norm2_g, w_in, conv_w, pool_w, pool_scale,
              na_rpb, w_out, router_w, router_b, moe_w1, moe_w3, moe_w2, final_g):
    b, n, d = x.shape
    l = ctx.shape[1]
    xc = ctx
    c_act = jax.nn.silu(c)
    cc_act = jax.nn.silu(c_ctx)
    for layer in range(DEPTH):
        last = layer == DEPTH - 1
        mod = c_act @ ada_w[layer] + ada_b[layer]
        mod_c = cc_act @ ada_w[layer] + ada_b[layer]
        sh1, sc1, g1, sh2, sc2, g2 = jnp.split(mod[:, None, :], 6, axis=-1)
        csh1, csc1, cg1, csh2, csc2, cg2 = jnp.split(mod_c, 6, axis=-1)

        h = rms_norm(x, norm1_g[layer]) * (1 + sc1) + sh1
        hc = rms_norm(xc, norm1_g[layer]) * (1 + csc1) + csh1
        u = h @ w_in[layer]
        a_v, a_b, a_c, p_in, q, k, v, f_in = jnp.split(u, N_PROJ_CHUNKS, axis=-1)
        if last:
            kc, vc = jnp.split(hc @ w_in[layer][:, 5 * GROUP_WIDTH:7 * GROUP_WIDTH], 2, axis=-1)
        else:
            uc = hc @ w_in[layer]
            ca_v, ca_b, ca_c, cp_in, cq, kc, vc, cf_in = jnp.split(uc, N_PROJ_CHUNKS, axis=-1)

        y = jnp.concatenate([
            gated_short_conv(a_v, a_b, a_c, conv_w[layer]),
            multiscale_pool(p_in, pool_w[layer], pool_scale[layer]),
            neighborhood_attention(q, k, v, kc, vc, na_rpb[layer]),
            fourier_mix(f_in),
        ], axis=-1)
        x = x + g1 * (y @ w_out[layer])
        if not last:
            yc = jnp.concatenate([
                gated_short_conv(ca_v, ca_b, ca_c, conv_w[layer]),
                multiscale_pool(cp_in, pool_w[layer], pool_scale[layer]),
                context_attention(cq, kc, vc),
                fourier_mix(cf_in),
            ], axis=-1)
            xc = xc + cg1 * (yc @ w_out[layer])

        h2 = rms_norm(x, norm2_g[layer]) * (1 + sc2) + sh2
        tokens = h2.reshape(b * n, d)
        if not last:
            h2c = rms_norm(xc, norm2_g[layer]) * (1 + csc2) + csh2
            tokens = jnp.concatenate([tokens, h2c.reshape(b * l, d)], axis=0)
        f = moe_ffn(tokens, router_w, router_b, moe_w1[layer], moe_w3[layer], moe_w2[layer])
        x = x + g2 * f[:b * n].reshape(b, n, d)
        if not last:
            xc = xc + cg2 * f[b * n:].reshape(b, l, d)
    return rms_norm(x, final_g)
```

```python
import functools
import math

import numpy as np
import jax
import jax.numpy as jnp
from jax import lax
from jax.experimental import pallas as pl
from jax.experimental.pallas import tpu as pltpu

F32 = jnp.float32
BF16 = jnp.bfloat16

GRID_W = 64
GROUP_WIDTH = 256
HEAD_DIM = 64
N_HEADS = GROUP_WIDTH // HEAD_DIM
CONV_K = 3
POOL_WINDOWS = (2, 4, 8, 16)
POOL_GROUP = GROUP_WIDTH // len(POOL_WINDOWS)
WIN_ROWS = 8
WIN_COLS = 16
N_EXPERTS = 16
EXPERTS_PER_GROUP = 4
N_EXPERT_GROUPS = N_EXPERTS // EXPERTS_PER_GROUP
EPS = 1e-6
NEG = -1e30

HALO = 16
NA_ROWS = 8
NA_BAND_BLOCKS = 4
NA_KBLK = 4 * GRID_W
LANES = 128

VMEM_LIMIT = 56 * 1024 * 1024


def _cparams(sem):
    return pltpu.CompilerParams(dimension_semantics=sem, vmem_limit_bytes=VMEM_LIMIT)


def _split(x):
    hi = x.astype(BF16)
    lo = (x - hi.astype(F32)).astype(BF16)
    return hi, lo


def _dot(a, b):
    return jnp.dot(a, b, preferred_element_type=F32)


def _dot_nt(a, b):
    return lax.dot_general(a, b, (((1,), (1,)), ((), ())), preferred_element_type=F32)


def _dot3(a, b):
    ah, al = _split(a)
    bh, bl = _split(b)
    return _dot(ah, bh) + _dot(al, bh) + _dot(ah, bl)


def _mod_kernel(c_ref, w_ref, b_ref, o_ref):
    c = c_ref[...]
    act = c / (1.0 + jnp.exp(-c))
    o_ref[0] = _dot3(act, w_ref[0]) + b_ref[0]


def _modulation(c_all, ada_w, ada_b):
    depth, d, n6 = ada_w.shape
    rows = c_all.shape[0]
    tn = 1536
    return pl.pallas_call(
        _mod_kernel,
        grid=(depth, n6 // tn),
        in_specs=[
            pl.BlockSpec((rows, d), lambda l, j: (0, 0)),
            pl.BlockSpec((1, d, tn), lambda l, j: (l, 0, j)),
            pl.BlockSpec((1, 1, tn), lambda l, j: (l, 0, j)),
        ],
        out_specs=pl.BlockSpec((1, rows, tn), lambda l, j: (l, 0, j)),
        out_shape=jax.ShapeDtypeStruct((depth, rows, n6), F32),
        compiler_params=_cparams(("arbitrary", "arbitrary")),
        name="adaln_mod",
    )(c_all, ada_w, ada_b.reshape(depth, 1, n6))


def _modulated_norm(x, gn, sc, sh):
    ms = jnp.mean(x * x, axis=-1, keepdims=True)
    return x * lax.rsqrt(ms + EPS) * (gn * (1.0 + sc)) + sh


def _win_kernel(*refs, n_main, has_dft, has_resid):
    it = iter(refs)
    x_ref = next(it)
    if has_resid:
        f_ref = next(it)
        g2_ref = next(it)
    gn_ref, sc_ref, sh_ref, w_ref = next(it), next(it), next(it), next(it)
    if has_dft:
        wd_ref = next(it)
    u_ref = next(it)
    if has_dft:
        g_ref = next(it)
    if has_resid:
        xo_ref = next(it)

    x = x_ref[0]
    if has_resid:
        x = x + g2_ref[0] * f_ref[0]
        xo_ref[0] = x
    h = _modulated_norm(x, gn_ref[...], sc_ref[0], sh_ref[0]).astype(BF16)
    u = _dot(h, w_ref[...])
    u_ref[0] = u[:, :n_main].astype(BF16)
    if has_dft:
        g_ref[0] = _dot(u[:, n_main:].astype(BF16), wd_ref[...]).astype(BF16)


def _input_projection(x, gn, sc, sh, w, wd=None, resid=None, tm=512):
    b, n, d = x.shape
    tm = min(tm, n)
    ncols = w.shape[1]
    has_dft = wd is not None
    has_resid = resid is not None
    n_main = ncols - GROUP_WIDTH if has_dft else ncols
    tok = pl.BlockSpec((1, tm, d), lambda i, j: (i, j, 0))
    vec = pl.BlockSpec((1, 1, d), lambda i, j: (i, 0, 0))
    args, specs = [x], [tok]
    if has_resid:
        args += [resid[0], resid[1]]
        specs += [tok, vec]
    args += [gn.reshape(1, d), sc, sh, w]
    specs += [pl.BlockSpec((1, d), lambda i, j: (0, 0)), vec, vec,
              pl.BlockSpec((d, ncols), lambda i, j: (0, 0))]
    if has_dft:
        args.append(wd)
        specs.append(pl.BlockSpec(wd.shape, lambda i, j: (0, 0)))
    out_shape = [jax.ShapeDtypeStruct((b, n, n_main), BF16)]
    out_specs = [pl.BlockSpec((1, tm, n_main), lambda i, j: (i, j, 0))]
    if has_dft:
        out_shape.append(jax.ShapeDtypeStruct((b, n, 2 * GROUP_WIDTH), BF16))
        out_specs.append(pl.BlockSpec((1, tm, 2 * GROUP_WIDTH), lambda i, j: (i, j, 0)))
    if has_resid:
        out_shape.append(jax.ShapeDtypeStruct((b, n, d), F32))
        out_specs.append(tok)
    return pl.pallas_call(
        functools.partial(_win_kernel, n_main=n_main, has_dft=has_dft, has_resid=has_resid),
        grid=(b, n // tm),
        in_specs=specs,
        out_specs=out_specs,
        out_shape=out_shape,
        compiler_params=_cparams(("arbitrary", "arbitrary")),
        name="norm_in_proj",
    )(*args)


def _convpool_kernel(av_ref, ab_ref, ac_ref, p_ref, avp_ref, acp_ref, pp_ref, avn_ref, acn_ref, pn_ref,
                     cw_ref, pw_ref, ps_ref, o_ref, *, rows, seq):
    j = pl.program_id(1)
    first = j == 0
    last = j == pl.num_programs(1) - 1
    ext_rows = rows + 2 * HALO

    def ext(main_ref, prev_ref, next_ref):
        prev = jnp.where(first, 0.0, prev_ref[0].astype(F32))
        nxt = jnp.where(last, 0.0, next_ref[0].astype(F32))
        return jnp.concatenate([prev, main_ref[0].astype(F32), nxt], axis=0)

    def shifted(a, s):
        return a if s % ext_rows == 0 else pltpu.roll(a, s % ext_rows, axis=0)

    def centre(a):
        return a[HALO:HALO + rows]

    z = ext(ac_ref, acp_ref, acn_ref) * ext(av_ref, avp_ref, avn_ref)
    cw = cw_ref[...]
    conv = cw[0:1] * shifted(z, 1) + cw[1:2] * z + cw[2:3] * shifted(z, -1)
    y_conv = ab_ref[0].astype(F32) * centre(conv)

    e = ext(p_ref, pp_ref, pn_ref)
    run = e
    sums = {}
    width = 1
    while width < max(POOL_WINDOWS):
        run = run + shifted(run, width)
        width *= 2
        sums[width] = run
    lane_grp = lax.broadcasted_iota(jnp.int32, (1, GROUP_WIDTH), 1) // POOL_GROUP
    t = j * rows + lax.broadcasted_iota(jnp.int32, (rows, 1), 0)
    win = None
    cnt = None
    for gi, w in enumerate(POOL_WINDOWS):
        ws = centre(shifted(sums[w], -(w // 2 - 1)))
        c = (jnp.minimum(t + w // 2, seq) - jnp.maximum(t - w // 2, 0)).astype(F32)
        if win is None:
            win, cnt = ws, jnp.broadcast_to(c, ws.shape)
        else:
            sel = lane_grp == gi
            win = jnp.where(sel, ws, win)
            cnt = jnp.where(sel, c, cnt)
    dlt = win / cnt - centre(e)
    y_pool = _dot(dlt.astype(BF16), pw_ref[...]) * ps_ref[...]
    o_ref[0] = jnp.concatenate([y_conv, y_pool], axis=1).astype(BF16)


def _conv_pool(u, conv_w, pool_bd, pool_scale, rows=512):
    b, n, _ = u.shape
    rows = min(rows, n)
    nh = n // HALO
    rb = rows // HALO
    gw = GROUP_WIDTH

    def main(col):
        return pl.BlockSpec((1, rows, gw), lambda i, j: (i, j, col))

    def prev(col):
        return pl.BlockSpec((1, HALO, gw), lambda i, j: (i, jnp.maximum(j * rb - 1, 0), col))

    def nxt(col):
        return pl.BlockSpec((1, HALO, gw), lambda i, j: (i, jnp.minimum((j + 1) * rb, nh - 1), col))

    small = lambda shape: pl.BlockSpec(shape, lambda i, j: (0, 0))
    return pl.pallas_call(
        functools.partial(_convpool_kernel, rows=rows, seq=n),
        grid=(b, n // rows),
        in_specs=[main(0), main(1), main(2), main(3),
                  prev(0), prev(2), prev(3), nxt(0), nxt(2), nxt(3),
                  small((CONV_K, gw)), small((gw, gw)), small((1, gw))],
        out_specs=pl.BlockSpec((1, rows, 2 * gw), lambda i, j: (i, j, 0)),
        out_shape=jax.ShapeDtypeStruct((b, n, 2 * gw), BF16),
        compiler_params=_cparams(("arbitrary", "arbitrary")),
        name="conv_pool",
    )(u, u, u, u, u, u, u, u, u, u, conv_w, pool_bd, pool_scale.reshape(1, gw))


def _na_geometry(grid_rows):
    n_groups = grid_rows // NA_ROWS
    band_rows = 4 * NA_BAND_BLOCKS
    kr = min(WIN_ROWS, grid_rows)
    starts, patterns = [], []
    for g in range(n_groups):
        s_blk = int(np.clip(2 * g - 1, 0, grid_rows // 4 - NA_BAND_BLOCKS))
        starts.append(s_blk)
        pat = np.full((NA_ROWS, band_rows), 2 * WIN_ROWS - 1, np.int32)
        for rl in range(NA_ROWS):
            r = g * NA_ROWS + rl
            r0 = int(np.clip(r - kr // 2, 0, grid_rows - kr))
            for jj in range(band_rows):
                a = 4 * s_blk + jj
                if r0 <= a < r0 + kr:
                    pat[rl, jj] = a - r + (WIN_ROWS - 1)
        patterns.append(pat)
    classes, cls_of_g = [], []
    for pat in patterns:
        for ci, c in enumerate(classes):
            if np.array_equal(c, pat):
                cls_of_g.append(ci)
                break
        else:
            classes.append(pat)
            cls_of_g.append(len(classes) - 1)
    return starts, cls_of_g, np.stack(classes)


def _na_bias_table(rpb, row_patterns):
    col = np.arange(GRID_W)
    c0 = np.clip(col - WIN_COLS // 2, 0, GRID_W - WIN_COLS)
    rel = col[None, :] - col[:, None] + (WIN_COLS - 1)
    valid = (col[None, :] >= c0[:, None]) & (col[None, :] < c0[:, None] + WIN_COLS)
    rel = np.clip(rel, 0, 2 * WIN_COLS - 2)
    t1 = jnp.take(rpb.astype(F32), jnp.asarray(rel.reshape(-1)), axis=2)
    t1 = t1.reshape(N_HEADS, 2 * WIN_ROWS - 1, GRID_W, GRID_W)
    t1 = jnp.where(jnp.asarray(valid)[None, None], t1, NEG)
    t1 = jnp.concatenate([t1, jnp.full((N_HEADS, 1, GRID_W, GRID_W), NEG, F32)], axis=1)
    ncls, nq, nb = row_patterns.shape
    tbl = jnp.take(t1, jnp.asarray(row_patterns.reshape(-1)), axis=1)
    tbl = tbl.reshape(N_HEADS, ncls, nq, nb, GRID_W, GRID_W).transpose(0, 1, 2, 4, 3, 5)
    return tbl.reshape(N_HEADS, ncls, nq * GRID_W, nb * GRID_W)


def _attend(q, keys, vals, biases):
    qs = (q.astype(F32) * (HEAD_DIM ** -0.5)).astype(BF16)
    lane_head = lax.broadcasted_iota(jnp.int32, (1, GROUP_WIDTH), 1) // HEAD_DIM
    out = jnp.zeros((q.shape[0], GROUP_WIDTH), F32)
    for h in range(N_HEADS):
        mh = lane_head == h
        qh = jnp.where(mh, qs, jnp.zeros_like(qs))
        s = []
        for kb, bias in zip(keys, biases):
            sb = _dot_nt(qh, kb)
            s.append(sb if bias is None else sb + bias(h))
        m = functools.reduce(jnp.maximum, [jnp.max(sb, axis=-1, keepdims=True) for sb in s])
        p = [jnp.exp(sb - m) for sb in s]
        l = functools.reduce(lambda a, c: a + c, [jnp.sum(pb, axis=-1, keepdims=True) for pb in p])
        o = functools.reduce(lambda a, c: a + c, [_dot(pb.astype(BF16), vb) for pb, vb in zip(p, vals)])
        out = jnp.where(mh, o * (1.0 / l), out)
    return out


def _na_kernel(*refs):
    q_ref = refs[0]
    k_refs = refs[1:1 + NA_BAND_BLOCKS]
    v_refs = refs[1 + NA_BAND_BLOCKS:1 + 2 * NA_BAND_BLOCKS]
    kc_ref, vc_ref, tbl_ref, o_ref = refs[1 + 2 * NA_BAND_BLOCKS:]
    keys = [r[0] for r in k_refs] + [kc_ref[0]]
    vals = [r[0] for r in v_refs] + [vc_ref[0]]
    biases = [(lambda h, jj=jj: tbl_ref[h, 0, :, jj * NA_KBLK:(jj + 1) * NA_KBLK]) for jj in range(NA_BAND_BLOCKS)]
    o_ref[0] = _attend(q_ref[0], keys, vals, biases + [None]).astype(BF16)


def _neighborhood_attention(u, uc, kc_col, rpb):
    b, n, _ = u.shape
    l = uc.shape[1]
    grid_rows = n // GRID_W
    starts, cls_of_g, row_patterns = _na_geometry(grid_rows)
    tbl = _na_bias_table(rpb, row_patterns)
    nq = NA_ROWS * GRID_W
    gw = GROUP_WIDTH

    def lookup(g, table):
        out = jnp.int32(table[0])
        for gi in range(1, len(table)):
            out = jnp.where(g == gi, jnp.int32(table[gi]), out)
        return out

    def band(col, jj):
        return pl.BlockSpec((1, NA_KBLK, gw), lambda g, i: (i, lookup(g, starts) + jj, col))

    return pl.pallas_call(
        _na_kernel,
        grid=(grid_rows // NA_ROWS, b),
        in_specs=[pl.BlockSpec((1, nq, gw), lambda g, i: (i, g, 4))]
        + [band(5, jj) for jj in range(NA_BAND_BLOCKS)]
        + [band(6, jj) for jj in range(NA_BAND_BLOCKS)]
        + [pl.BlockSpec((1, l, gw), lambda g, i: (i, 0, kc_col)),
           pl.BlockSpec((1, l, gw), lambda g, i: (i, 0, kc_col + 1)),
           pl.BlockSpec((N_HEADS, 1, nq, NA_BAND_BLOCKS * NA_KBLK), lambda g, i: (0, lookup(g, cls_of_g), 0, 0))],
        out_specs=pl.BlockSpec((1, nq, gw), lambda g, i: (i, g, 0)),
        out_shape=jax.ShapeDtypeStruct((b, n, gw), BF16),
        compiler_params=_cparams(("arbitrary", "arbitrary")),
        name="neighborhood_attn",
    )(u, *([u] * (2 * NA_BAND_BLOCKS)), uc, uc, tbl)


def _ctx_attn_kernel(q_ref, k_ref, v_ref, o_ref):
    o_ref[0] = _attend(q_ref[0], [k_ref[0]], [v_ref[0]], [None]).astype(BF16)


def _context_attention(uc):
    b, l, _ = uc.shape
    gw = GROUP_WIDTH
    return pl.pallas_call(
        _ctx_attn_kernel,
        grid=(b,),
        in_specs=[pl.BlockSpec((1, l, gw), lambda i, col=col: (i, 0, col)) for col in (4, 5, 6)],
        out_specs=pl.BlockSpec((1, l, gw), lambda i: (i, 0, 0)),
        out_shape=jax.ShapeDtypeStruct((b, l, gw), BF16),
        compiler_params=_cparams(("arbitrary",)),
        name="context_attn",
    )(uc, uc, uc)


def _channel_dft():
    j = np.arange(HEAD_DIM)
    ang = 2.0 * np.pi * ((j[:, None] * j[None, :]) % HEAD_DIM) / HEAD_DIM
    eye = np.eye(N_HEADS)
    c = np.kron(eye, np.cos(ang)) / math.sqrt(HEAD_DIM)
    s = np.kron(eye, np.sin(ang)) / math.sqrt(HEAD_DIM)
    return jnp.asarray(np.concatenate([c, s], axis=1), F32).astype(BF16)


def _position_dft(n):
    r = int(round(math.sqrt(n)))
    if n <= 256 or r * r != n:
        t = np.arange(n)
        ang = 2.0 * np.pi * ((t[:, None] * t[None, :]) % n) / n
        c = jnp.asarray(np.cos(ang) / math.sqrt(n), F32)
        s = jnp.asarray(-np.sin(ang) / math.sqrt(n), F32)
        return c.astype(BF16), s.astype(BF16)
    i = np.arange(r)
    a = 2.0 * np.pi * ((i[:, None] * i[None, :]) % r) / r
    bb = 2.0 * np.pi * (((i[:, None, None] * i[None, :, None] * r) + i[:, None, None] * i[None, None, :]) % n) / n
    ca = jnp.asarray(np.cos(a), F32)[:, None, None, :]
    sa = jnp.asarray(np.sin(a), F32)[:, None, None, :]
    cb = jnp.asarray(np.cos(bb) / math.sqrt(n), F32)[None]
    sb = jnp.asarray(np.sin(bb) / math.sqrt(n), F32)[None]
    c = (ca * cb - sa * sb).reshape(n, n)
    s = (-(sa * cb + ca * sb)).reshape(n, n)
    return c.astype(BF16), s.astype(BF16)


def _dft_kernel(c_ref, s_ref, g_ref, o_ref, acc_ref):
    k = pl.program_id(1)

    @pl.when(k == 0)
    def _():
        acc_ref[...] = jnp.zeros_like(acc_ref)

    c = c_ref[...]
    s = s_ref[...]
    for i in range(g_ref.shape[0]):
        g = g_ref[i]
        acc_ref[i] += _dot(c, g[:, :GROUP_WIDTH]) + _dot(s, g[:, GROUP_WIDTH:])

    @pl.when(k == pl.num_programs(1) - 1)
    def _():
        o_ref[...] = acc_ref[...].astype(BF16)


def _position_dft_apply(g, cmat, smat, tm=1024, tk=512):
    b, n, _ = g.shape
    tm, tk = min(tm, n), min(tk, n)
    gw = GROUP_WIDTH
    return pl.pallas_call(
        _dft_kernel,
        grid=(n // tm, n // tk),
        in_specs=[pl.BlockSpec((tm, tk), lambda i, k: (i, k)),
                  pl.BlockSpec((tm, tk), lambda i, k: (i, k)),
                  pl.BlockSpec((b, tk, 2 * gw), lambda i, k: (0, k, 0))],
        out_specs=pl.BlockSpec((b, tm, gw), lambda i, k: (0, i, 0)),
        out_shape=jax.ShapeDtypeStruct((b, n, gw), BF16),
        scratch_shapes=[pltpu.VMEM((b, tm, gw), F32)],
        compiler_params=_cparams(("arbitrary", "arbitrary")),
        name="position_dft",
    )(cmat, smat, g)


def _router_gates(logits_t, rb_ref):
    m = logits_t.shape[1]
    score, biased = [], []
    for e in range(N_EXPERTS):
        le = logits_t[e:e + 1, :]
        se = 1.0 / (1.0 + jnp.exp(-le))
        score.append(se)
        biased.append(se + rb_ref[e:e + 1, :])
    npg = EXPERTS_PER_GROUP
    grp_score = []
    for g in range(N_EXPERT_GROUPS):
        bg = biased[g * npg:(g + 1) * npg]
        pairs = [bg[i] + bg[j] for i in range(npg) for j in range(i + 1, npg)]
        grp_score.append(functools.reduce(jnp.maximum, pairs))
    best = jnp.zeros((1, m), jnp.int32)
    cur = grp_score[0]
    for g in range(1, N_EXPERT_GROUPS):
        upd = grp_score[g] > cur
        best = jnp.where(upd, g, best)
        cur = jnp.where(upd, grp_score[g], cur)
    vb, vu = [], []
    for i in range(npg):
        b_i, u_i = biased[i], score[i]
        for g in range(1, N_EXPERT_GROUPS):
            sel = best == g
            b_i = jnp.where(sel, biased[g * npg + i], b_i)
            u_i = jnp.where(sel, score[g * npg + i], u_i)
        vb.append(b_i)
        vu.append(u_i)
    i1 = jnp.zeros((1, m), jnp.int32)
    c1, u1 = vb[0], vu[0]
    for i in range(1, npg):
        upd = vb[i] > c1
        i1 = jnp.where(upd, i, i1)
        c1 = jnp.where(upd, vb[i], c1)
        u1 = jnp.where(upd, vu[i], u1)
    i2 = jnp.full((1, m), -1, jnp.int32)
    c2 = jnp.full((1, m), -jnp.inf, F32)
    u2 = jnp.zeros((1, m), F32)
    for i in range(npg):
        upd = (i1 != i) & (vb[i] > c2)
        i2 = jnp.where(upd, i, i2)
        c2 = jnp.where(upd, vb[i], c2)
        u2 = jnp.where(upd, vu[i], u2)
    tot = u1 + u2
    w1, w2 = u1 / tot, u2 / tot
    e1 = best * npg + i1
    e2 = best * npg + i2
    rows = []
    for e in range(N_EXPERTS):
        rows.append(jnp.where(e1 == e, w1, 0.0) + jnp.where(e2 == e, w2, 0.0))
    return jnp.concatenate(rows, axis=0)


def _wout_kernel(ycp_ref, ya_ref, yf_ref, x_ref, g1_ref, w_ref, gn_ref, sc_ref, sh_ref,
                 rwa_ref, rwb_ref, rb_ref, x1_ref, h2_ref, gate_ref):
    gw = GROUP_WIDTH
    mix = (_dot(ycp_ref[0], w_ref[0:2 * gw, :]) + _dot(ya_ref[0], w_ref[2 * gw:3 * gw, :])
           + _dot(yf_ref[0], w_ref[3 * gw:4 * gw, :]))
    x1 = x_ref[0] + g1_ref[0] * mix
    x1_ref[0] = x1
    h2 = _modulated_norm(x1, gn_ref[...], sc_ref[0], sh_ref[0])
    h2_ref[0] = h2.astype(BF16)
    hh, hl = _split(h2)
    la = _dot(hh, rwa_ref[...]).T
    lb = _dot(hl, rwb_ref[...]).T
    logits_t = la[0:N_EXPERTS] + la[N_EXPERTS:2 * N_EXPERTS] + lb[0:N_EXPERTS]
    gates_t = _router_gates(logits_t, rb_ref)
    pad = jnp.zeros((LANES - N_EXPERTS, gates_t.shape[1]), F32)
    gate_ref[0] = jnp.concatenate([gates_t, pad], axis=0).T


def _output_projection(ycp, ya, yf, x, g1, w_out, gn, sc, sh, rwa, rwb, rb, tm=512):
    b, n, d = x.shape
    tm = min(tm, n)
    gw = GROUP_WIDTH
    tokb = lambda width: pl.BlockSpec((1, tm, width), lambda i, j: (i, j, 0))
    vec = pl.BlockSpec((1, 1, d), lambda i, j: (i, 0, 0))
    full = lambda shape: pl.BlockSpec(shape, lambda i, j: (0, 0))
    return pl.pallas_call(
        _wout_kernel,
        grid=(b, n // tm),
        in_specs=[tokb(2 * gw), tokb(gw), tokb(gw), tokb(d), vec, full((4 * gw, d)), full((1, d)), vec, vec,
                  full((d, LANES)), full((d, LANES)), full((N_EXPERTS, 1))],
        out_specs=[tokb(d), tokb(d), tokb(LANES)],
        out_shape=[jax.ShapeDtypeStruct((b, n, d), F32), jax.ShapeDtypeStruct((b, n, d), BF16),
                   jax.ShapeDtypeStruct((b, n, LANES), F32)],
        compiler_params=_cparams(("arbitrary", "arbitrary")),
        name="out_proj_router",
    )(ycp, ya, yf, x, g1, w_out, gn.reshape(1, d), sc, sh, rwa, rwb, rb)


def _moe_kernel(h_ref, gate_ref, w1_ref, w3_ref, w2_ref, o_ref, acc_ref):
    e = pl.program_id(1)

    @pl.when(e == 0)
    def _():
        acc_ref[...] = jnp.zeros_like(acc_ref)

    h = h_ref[...]
    a = _dot(h, w1_ref[0])
    he = (a / (1.0 + jnp.exp(-a))) * _dot(h, w3_ref[0])
    y = _dot(he.astype(BF16), w2_ref[0])
    lane = lax.broadcasted_iota(jnp.int32, (1, LANES), 1)
    ge = jnp.sum(jnp.where(lane == e, gate_ref[...], 0.0), axis=-1, keepdims=True)
    acc_ref[...] += ge * y

    @pl.when(e == pl.num_programs(1) - 1)
    def _():
        o_ref[...] = acc_ref[...]


def _moe(h2, gates, w1, w3, w2, tm=1024):
    t, d = h2.shape
    tm = min(tm, t)
    ne, _, hid = w1.shape
    return pl.pallas_call(
        _moe_kernel,
        grid=(t // tm, ne),
        in_specs=[pl.BlockSpec((tm, d), lambda i, e: (i, 0)),
                  pl.BlockSpec((tm, LANES), lambda i, e: (i, 0)),
                  pl.BlockSpec((1, d, hid), lambda i, e: (e, 0, 0)),
                  pl.BlockSpec((1, d, hid), lambda i, e: (e, 0, 0)),
                  pl.BlockSpec((1, hid, d), lambda i, e: (e, 0, 0))],
        out_specs=pl.BlockSpec((tm, d), lambda i, e: (i, 0)),
        out_shape=jax.ShapeDtypeStruct((t, d), F32),
        scratch_shapes=[pltpu.VMEM((tm, d), F32)],
        compiler_params=_cparams(("arbitrary", "arbitrary")),
        name="moe_dense",
    )(h2, gates, w1, w3, w2)


def _final_kernel(x_ref, f_ref, g2_ref, gn_ref, o_ref):
    x = x_ref[0] + g2_ref[0] * f_ref[0]
    ms = jnp.mean(x * x, axis=-1, keepdims=True)
    o_ref[0] = x * lax.rsqrt(ms + EPS) * gn_ref[...]


def _final_norm(x1, f, g2, gn, tm=1024):
    b, n, d = x1.shape
    tm = min(tm, n)
    tok = pl.BlockSpec((1, tm, d), lambda i, j: (i, j, 0))
    return pl.pallas_call(
        _final_kernel,
        grid=(b, n // tm),
        in_specs=[tok, tok, pl.BlockSpec((1, 1, d), lambda i, j: (i, 0, 0)), pl.BlockSpec((1, d), lambda i, j: (0, 0))],
        out_specs=tok,
        out_shape=jax.ShapeDtypeStruct((b, n, d), F32),
        compiler_params=_cparams(("arbitrary", "arbitrary")),
        name="final_norm",
    )(x1, f, g2, gn.reshape(1, d))


def _block_diag(blocks):
    g, c, _ = blocks.shape
    out = jnp.zeros((g * c, g * c), blocks.dtype)
    for i in range(g):
        out = lax.dynamic_update_slice(out, blocks[i], (i * c, i * c))
    return out


def kernel(x, c, ctx, c_ctx, ada_w, ada_b, norm1_g, norm2_g, w_in, conv_w, pool_w, pool_scale, na_rpb, w_out,
           router_w, router_b, moe_w1, moe_w3, moe_w2, final_g):
    b, n, d = x.shape
    l = ctx.shape[1]
    depth = ada_w.shape[0]
    gw = GROUP_WIDTH

    rows = -(-(b + 1) // 8) * 8
    c_all = jnp.concatenate([c, c_ctx[None, :], jnp.zeros((rows - b - 1, d), F32)], axis=0)
    mod = _modulation(c_all, ada_w, ada_b)

    def lat_mod(layer, i):
        return mod[layer, :b, i * d:(i + 1) * d].reshape(b, 1, d)

    def ctx_mod(layer, i):
        return jnp.broadcast_to(mod[layer, b, i * d:(i + 1) * d].reshape(1, 1, d), (b, 1, d))

    wd = _channel_dft()
    cn, sn = _position_dft(n)
    cl, sl = _position_dft(l)

    rw_hi, rw_lo = _split(router_w.astype(F32))
    zpad = lambda k: jnp.zeros((d, LANES - k * N_EXPERTS), BF16)
    rwa = jnp.concatenate([rw_hi, rw_lo, zpad(2)], axis=1)
    rwb = jnp.concatenate([rw_hi, zpad(1)], axis=1)
    rb = router_b.astype(F32).reshape(N_EXPERTS, 1)

    xc = ctx
    resid = None
    resid_c = None
    for layer in range(depth):
        last = layer == depth - 1
        w_in_l = w_in[layer].astype(BF16)
        w_out_l = w_out[layer].astype(BF16)
        w1 = moe_w1[layer].astype(BF16)
        w3 = moe_w3[layer].astype(BF16)
        w2 = moe_w2[layer].astype(BF16)
        pool_bd = _block_diag(pool_w[layer].astype(BF16))

        outs = _input_projection(x, norm1_g[layer], lat_mod(layer, 1), lat_mod(layer, 0), w_in_l, wd=wd, resid=resid)
        u, g = outs[0], outs[1]
        if resid is not None:
            x = outs[2]
        if last:
            outs_c = _input_projection(xc, norm1_g[layer], ctx_mod(layer, 1), ctx_mod(layer, 0),
                                       w_in_l[:, 5 * gw:7 * gw], resid=resid_c)
            uc, kc_col = outs_c[0], 0
        else:
            outs_c = _input_projection(xc, norm1_g[layer], ctx_mod(layer, 1), ctx_mod(layer, 0), w_in_l, wd=wd,
                                       resid=resid_c)
            uc, gc, kc_col = outs_c[0], outs_c[1], 5
            if resid_c is not None:
                xc = outs_c[2]

        ycp = _conv_pool(u, conv_w[layer], pool_bd, pool_scale[layer])
        ya = _neighborhood_attention(u, uc, kc_col, na_rpb[layer])
        yf = _position_dft_apply(g, cn, sn)
        x1, h2, gates = _output_projection(ycp, ya, yf, x, lat_mod(layer, 2), w_out_l, norm2_g[layer],
                                           lat_mod(layer, 4), lat_mod(layer, 3), rwa, rwb, rb)
        f = _moe(h2.reshape(b * n, d), gates.reshape(b * n, LANES), w1, w3, w2).reshape(b, n, d)
        x = x1
        resid = (f, lat_mod(layer, 5))
        if not last:
            ycp_c = _conv_pool(uc, conv_w[layer], pool_bd, pool_scale[layer])
            ya_c = _context_attention(uc)
            yf_c = _position_dft_apply(gc, cl, sl)
            xc1, h2c, gates_c = _output_projection(ycp_c, ya_c, yf_c, xc, ctx_mod(layer, 2), w_out_l, norm2_g[layer],
                                                   ctx_mod(layer, 4), ctx_mod(layer, 3), rwa, rwb, rb)
            fc = _moe(h2c.reshape(b * l, d), gates_c.reshape(b * l, LANES), w1, w3, w2).reshape(b, l, d)
            xc = xc1
            resid_c = (fc, ctx_mod(layer, 5))
    return _final_norm(x, resid[0], resid[1], final_g)
```

```python
import functools
import math

import numpy as np
import jax
import jax.numpy as jnp
from jax import lax
from jax.experimental import pallas as pl
from jax.experimental.pallas import tpu as pltpu

F32 = jnp.float32
BF16 = jnp.bfloat16

GRID_W = 64
GROUP_WIDTH = 256
HEAD_DIM = 64
N_HEADS = GROUP_WIDTH // HEAD_DIM
CONV_K = 3
POOL_WINDOWS = (2, 4, 8, 16)
POOL_GROUP = GROUP_WIDTH // len(POOL_WINDOWS)
WIN_ROWS = 8
WIN_COLS = 16
N_EXPERTS = 16
EXPERTS_PER_GROUP = 4
N_EXPERT_GROUPS = N_EXPERTS // EXPERTS_PER_GROUP
EPS = 1e-6
NEG = -1e30

HALO = 16
NA_ROWS = 8
NA_BAND_BLOCKS = 4
NA_KBLK = 4 * GRID_W
LANES = 128

_PAIRS = [(i, j) for i in range(EXPERTS_PER_GROUP) for j in range(i + 1, EXPERTS_PER_GROUP)]
PAIRS_PER_GROUP = len(_PAIRS)
N_CLASSES = N_EXPERT_GROUPS * PAIRS_PER_GROUP
CLS_PAD = 32
CLASS_EA = [g * EXPERTS_PER_GROUP + p[0] for g in range(N_EXPERT_GROUPS) for p in _PAIRS]
CLASS_EB = [g * EXPERTS_PER_GROUP + p[1] for g in range(N_EXPERT_GROUPS) for p in _PAIRS]
ROW_CHUNK = 512

VMEM_LIMIT = 56 * 1024 * 1024


def _cparams(sem):
    return pltpu.CompilerParams(dimension_semantics=sem, vmem_limit_bytes=VMEM_LIMIT)


def _split(x):
    hi = x.astype(BF16)
    lo = (x - hi.astype(F32)).astype(BF16)
    return hi, lo


def _dot(a, b):
    return jnp.dot(a, b, preferred_element_type=F32)


def _dot_nt(a, b):
    return lax.dot_general(a, b, (((1,), (1,)), ((), ())), preferred_element_type=F32)


def _dot3(a, b):
    ah, al = _split(a)
    bh, bl = _split(b)
    return _dot(ah, bh) + _dot(al, bh) + _dot(ah, bl)


def _mod_kernel(c_ref, w_ref, b_ref, o_ref):
    c = c_ref[...]
    act = c / (1.0 + jnp.exp(-c))
    o_ref[0] = _dot3(act, w_ref[0]) + b_ref[0]


def _modulation(c_all, ada_w, ada_b):
    depth, d, n6 = ada_w.shape
    rows = c_all.shape[0]
    tn = 1536
    return pl.pallas_call(
        _mod_kernel,
        grid=(depth, n6 // tn),
        in_specs=[
            pl.BlockSpec((rows, d), lambda l, j: (0, 0)),
            pl.BlockSpec((1, d, tn), lambda l, j: (l, 0, j)),
            pl.BlockSpec((1, 1, tn), lambda l, j: (l, 0, j)),
        ],
        out_specs=pl.BlockSpec((1, rows, tn), lambda l, j: (l, 0, j)),
        out_shape=jax.ShapeDtypeStruct((depth, rows, n6), F32),
        compiler_params=_cparams(("arbitrary", "arbitrary")),
        name="adaln_mod",
    )(c_all, ada_w, ada_b.reshape(depth, 1, n6))


def _modulated_norm(x, gn, sc, sh):
    ms = jnp.mean(x * x, axis=-1, keepdims=True)
    return x * lax.rsqrt(ms + EPS) * (gn * (1.0 + sc)) + sh


def _win_kernel(*refs, n_main, has_dft, has_resid):
    it = iter(refs)
    x_ref = next(it)
    if has_resid:
        f_ref = next(it)
        g2_ref = next(it)
    gn_ref, sc_ref, sh_ref, w_ref = next(it), next(it), next(it), next(it)
    if has_dft:
        wd_ref = next(it)
    u_ref = next(it)
    if has_dft:
        g_ref = next(it)
    if has_resid:
        xo_ref = next(it)

    x = x_ref[0]
    if has_resid:
        x = x + g2_ref[0] * f_ref[0]
        xo_ref[0] = x
    h = _modulated_norm(x, gn_ref[...], sc_ref[0], sh_ref[0]).astype(BF16)
    u = _dot(h, w_ref[...])
    u_ref[0] = u[:, :n_main].astype(BF16)
    if has_dft:
        g_ref[0] = _dot(u[:, n_main:].astype(BF16), wd_ref[...]).astype(BF16)


def _input_projection(x, gn, sc, sh, w, wd=None, resid=None, tm=512):
    b, n, d = x.shape
    tm = min(tm, n)
    ncols = w.shape[1]
    has_dft = wd is not None
    has_resid = resid is not None
    n_main = ncols - GROUP_WIDTH if has_dft else ncols
    tok = pl.BlockSpec((1, tm, d), lambda i, j: (i, j, 0))
    vec = pl.BlockSpec((1, 1, d), lambda i, j: (i, 0, 0))
    args, specs = [x], [tok]
    if has_resid:
        args += [resid[0], resid[1]]
        specs += [tok, vec]
    args += [gn.reshape(1, d), sc, sh, w]
    specs += [pl.BlockSpec((1, d), lambda i, j: (0, 0)), vec, vec,
              pl.BlockSpec((d, ncols), lambda i, j: (0, 0))]
    if has_dft:
        args.append(wd)
        specs.append(pl.BlockSpec(wd.shape, lambda i, j: (0, 0)))
    out_shape = [jax.ShapeDtypeStruct((b, n, n_main), BF16)]
    out_specs = [pl.BlockSpec((1, tm, n_main), lambda i, j: (i, j, 0))]
    if has_dft:
        out_shape.append(jax.ShapeDtypeStruct((b, n, 2 * GROUP_WIDTH), BF16))
        out_specs.append(pl.BlockSpec((1, tm, 2 * GROUP_WIDTH), lambda i, j: (i, j, 0)))
    if has_resid:
        out_shape.append(jax.ShapeDtypeStruct((b, n, d), F32))
        out_specs.append(tok)
    return pl.pallas_call(
        functools.partial(_win_kernel, n_main=n_main, has_dft=has_dft, has_resid=has_resid),
        grid=(b, n // tm),
        in_specs=specs,
        out_specs=out_specs,
        out_shape=out_shape,
        compiler_params=_cparams(("arbitrary", "arbitrary")),
        name="norm_in_proj",
    )(*args)


def _convpool_kernel(av_ref, ab_ref, ac_ref, p_ref, avp_ref, acp_ref, pp_ref, avn_ref, acn_ref, pn_ref,
                     cw_ref, pw_ref, ps_ref, o_ref, *, rows, seq):
    j = pl.program_id(1)
    first = j == 0
    last = j == pl.num_programs(1) - 1
    ext_rows = rows + 2 * HALO

    def ext(main_ref, prev_ref, next_ref):
        prev = jnp.where(first, 0.0, prev_ref[0].astype(F32))
        nxt = jnp.where(last, 0.0, next_ref[0].astype(F32))
        return jnp.concatenate([prev, main_ref[0].astype(F32), nxt], axis=0)

    def shifted(a, s):
        return a if s % ext_rows == 0 else pltpu.roll(a, s % ext_rows, axis=0)

    def centre(a):
        return a[HALO:HALO + rows]

    z = ext(ac_ref, acp_ref, acn_ref) * ext(av_ref, avp_ref, avn_ref)
    cw = cw_ref[...]
    conv = cw[0:1] * shifted(z, 1) + cw[1:2] * z + cw[2:3] * shifted(z, -1)
    y_conv = ab_ref[0].astype(F32) * centre(conv)

    e = ext(p_ref, pp_ref, pn_ref)
    run = e
    sums = {}
    width = 1
    while width < max(POOL_WINDOWS):
        run = run + shifted(run, width)
        width *= 2
        sums[width] = run
    lane_grp = lax.broadcasted_iota(jnp.int32, (1, GROUP_WIDTH), 1) // POOL_GROUP
    t = j * rows + lax.broadcasted_iota(jnp.int32, (rows, 1), 0)
    win = None
    cnt = None
    for gi, w in enumerate(POOL_WINDOWS):
        ws = centre(shifted(sums[w], -(w // 2 - 1)))
        c = (jnp.minimum(t + w // 2, seq) - jnp.maximum(t - w // 2, 0)).astype(F32)
        if win is None:
            win, cnt = ws, jnp.broadcast_to(c, ws.shape)
        else:
            sel = lane_grp == gi
            win = jnp.where(sel, ws, win)
            cnt = jnp.where(sel, c, cnt)
    dlt = win / cnt - centre(e)
    y_pool = _dot(dlt.astype(BF16), pw_ref[...]) * ps_ref[...]
    o_ref[0] = jnp.concatenate([y_conv, y_pool], axis=1).astype(BF16)


def _conv_pool(u, conv_w, pool_bd, pool_scale, rows=512):
    b, n, _ = u.shape
    rows = min(rows, n)
    nh = n // HALO
    rb = rows // HALO
    gw = GROUP_WIDTH

    def main(col):
        return pl.BlockSpec((1, rows, gw), lambda i, j: (i, j, col))

    def prev(col):
        return pl.BlockSpec((1, HALO, gw), lambda i, j: (i, jnp.maximum(j * rb - 1, 0), col))

    def nxt(col):
        return pl.BlockSpec((1, HALO, gw), lambda i, j: (i, jnp.minimum((j + 1) * rb, nh - 1), col))

    small = lambda shape: pl.BlockSpec(shape, lambda i, j: (0, 0))
    return pl.pallas_call(
        functools.partial(_convpool_kernel, rows=rows, seq=n),
        grid=(b, n // rows),
        in_specs=[main(0), main(1), main(2), main(3),
                  prev(0), prev(2), prev(3), nxt(0), nxt(2), nxt(3),
                  small((CONV_K, gw)), small((gw, gw)), small((1, gw))],
        out_specs=pl.BlockSpec((1, rows, 2 * gw), lambda i, j: (i, j, 0)),
        out_shape=jax.ShapeDtypeStruct((b, n, 2 * gw), BF16),
        compiler_params=_cparams(("arbitrary", "arbitrary")),
        name="conv_pool",
    )(u, u, u, u, u, u, u, u, u, u, conv_w, pool_bd, pool_scale.reshape(1, gw))


def _na_geometry(grid_rows):
    n_groups = grid_rows // NA_ROWS
    band_rows = 4 * NA_BAND_BLOCKS
    kr = min(WIN_ROWS, grid_rows)
    starts, patterns = [], []
    for g in range(n_groups):
        s_blk = int(np.clip(2 * g - 1, 0, grid_rows // 4 - NA_BAND_BLOCKS))
        starts.append(s_blk)
        pat = np.full((NA_ROWS, band_rows), 2 * WIN_ROWS - 1, np.int32)
        for rl in range(NA_ROWS):
            r = g * NA_ROWS + rl
            r0 = int(np.clip(r - kr // 2, 0, grid_rows - kr))
            for jj in range(band_rows):
                a = 4 * s_blk + jj
                if r0 <= a < r0 + kr:
                    pat[rl, jj] = a - r + (WIN_ROWS - 1)
        patterns.append(pat)
    pairs = np.stack(patterns).reshape(n_groups, NA_ROWS, band_rows // 2, 2)
    slab_pairs, slab_of = np.unique(pairs.reshape(-1, 2), axis=0, return_inverse=True)
    return starts, slab_pairs, slab_of.reshape(-1).astype(np.int32)


def _na_bias_slabs(rpb, slab_pairs):
    col = np.arange(GRID_W)
    c0 = np.clip(col - WIN_COLS // 2, 0, GRID_W - WIN_COLS)
    rel = col[None, :] - col[:, None] + (WIN_COLS - 1)
    valid = (col[None, :] >= c0[:, None]) & (col[None, :] < c0[:, None] + WIN_COLS)
    rel = np.clip(rel, 0, 2 * WIN_COLS - 2)
    t1 = jnp.take(rpb.astype(F32), jnp.asarray(rel.reshape(-1)), axis=2)
    t1 = t1.reshape(N_HEADS, 2 * WIN_ROWS - 1, GRID_W, GRID_W)
    t1 = jnp.where(jnp.asarray(valid)[None, None], t1, NEG)
    t1 = jnp.concatenate([t1, jnp.full((N_HEADS, 1, GRID_W, GRID_W), NEG, F32)], axis=1)
    return jnp.concatenate([jnp.take(t1, jnp.asarray(slab_pairs[:, 0]), axis=1),
                            jnp.take(t1, jnp.asarray(slab_pairs[:, 1]), axis=1)], axis=-1)


def _attend(q, keys, vals, biases):
    qs = (q.astype(F32) * (HEAD_DIM ** -0.5)).astype(BF16)
    lane_head = lax.broadcasted_iota(jnp.int32, (1, GROUP_WIDTH), 1) // HEAD_DIM
    out = jnp.zeros((q.shape[0], GROUP_WIDTH), F32)
    for h in range(N_HEADS):
        mh = lane_head == h
        qh = jnp.where(mh, qs, jnp.zeros_like(qs))
        s = []
        for kb, bias in zip(keys, biases):
            sb = _dot_nt(qh, kb)
            s.append(sb if bias is None else sb + bias(h))
        m = functools.reduce(jnp.maximum, [jnp.max(sb, axis=-1, keepdims=True) for sb in s])
        p = [jnp.exp(sb - m) for sb in s]
        l = functools.reduce(lambda a, c: a + c, [jnp.sum(pb, axis=-1, keepdims=True) for pb in p])
        o = functools.reduce(lambda a, c: a + c, [_dot(pb.astype(BF16), vb) for pb, vb in zip(p, vals)])
        out = jnp.where(mh, o * (1.0 / l), out)
    return out


def _na_kernel(start_ref, slab_ref, *refs):
    del start_ref
    q_ref = refs[0]
    k_refs = refs[1:1 + NA_BAND_BLOCKS]
    v_refs = refs[1 + NA_BAND_BLOCKS:1 + 2 * NA_BAND_BLOCKS]
    kc_ref, vc_ref, tbl_ref, o_ref = refs[1 + 2 * NA_BAND_BLOCKS:]
    keys = [r[0] for r in k_refs] + [kc_ref[0]]
    vals = [r[0] for r in v_refs] + [vc_ref[0]]
    slabs_per_row = 2 * NA_BAND_BLOCKS
    base = pl.program_id(0) * (NA_ROWS * slabs_per_row)

    def bias(h, blk):
        rows = []
        for rl in range(NA_ROWS):
            idx = base + rl * slabs_per_row + 2 * blk
            rows.append(jnp.concatenate([tbl_ref[h, slab_ref[idx]], tbl_ref[h, slab_ref[idx + 1]]], axis=1))
        return jnp.concatenate(rows, axis=0)

    biases = [functools.partial(bias, blk=blk) for blk in range(NA_BAND_BLOCKS)]
    o_ref[0] = _attend(q_ref[0], keys, vals, biases + [None]).astype(BF16)


def _neighborhood_attention(u, uc, kc_col, rpb):
    b, n, _ = u.shape
    l = uc.shape[1]
    grid_rows = n // GRID_W
    starts, slab_pairs, slab_of = _na_geometry(grid_rows)
    tbl = _na_bias_slabs(rpb, slab_pairs)
    nq = NA_ROWS * GRID_W
    gw = GROUP_WIDTH

    def band(col, jj):
        return pl.BlockSpec((1, NA_KBLK, gw), lambda g, i, st, sl: (i, st[g] + jj, col))

    return pl.pallas_call(
        _na_kernel,
        grid_spec=pltpu.PrefetchScalarGridSpec(
            num_scalar_prefetch=2,
            grid=(grid_rows // NA_ROWS, b),
            in_specs=[pl.BlockSpec((1, nq, gw), lambda g, i, st, sl: (i, g, 4))]
            + [band(5, jj) for jj in range(NA_BAND_BLOCKS)]
            + [band(6, jj) for jj in range(NA_BAND_BLOCKS)]
            + [pl.BlockSpec((1, l, gw), lambda g, i, st, sl: (i, 0, kc_col)),
               pl.BlockSpec((1, l, gw), lambda g, i, st, sl: (i, 0, kc_col + 1)),
               pl.BlockSpec(tbl.shape, lambda g, i, st, sl: (0, 0, 0, 0))],
            out_specs=pl.BlockSpec((1, nq, gw), lambda g, i, st, sl: (i, g, 0)),
        ),
        out_shape=jax.ShapeDtypeStruct((b, n, gw), BF16),
        compiler_params=_cparams(("arbitrary", "arbitrary")),
        name="neighborhood_attn",
    )(jnp.asarray(starts, jnp.int32), jnp.asarray(slab_of), u, *([u] * (2 * NA_BAND_BLOCKS)), uc, uc, tbl)


def _ctx_attn_kernel(q_ref, k_ref, v_ref, o_ref):
    o_ref[0] = _attend(q_ref[0], [k_ref[0]], [v_ref[0]], [None]).astype(BF16)


def _context_attention(uc):
    b, l, _ = uc.shape
    gw = GROUP_WIDTH
    return pl.pallas_call(
        _ctx_attn_kernel,
        grid=(b,),
        in_specs=[pl.BlockSpec((1, l, gw), lambda i, col=col: (i, 0, col)) for col in (4, 5, 6)],
        out_specs=pl.BlockSpec((1, l, gw), lambda i: (i, 0, 0)),
        out_shape=jax.ShapeDtypeStruct((b, l, gw), BF16),
        compiler_params=_cparams(("arbitrary",)),
        name="context_attn",
    )(uc, uc, uc)


def _channel_dft():
    j = np.arange(HEAD_DIM)
    ang = 2.0 * np.pi * ((j[:, None] * j[None, :]) % HEAD_DIM) / HEAD_DIM
    eye = np.eye(N_HEADS)
    c = np.kron(eye, np.cos(ang)) / math.sqrt(HEAD_DIM)
    s = np.kron(eye, np.sin(ang)) / math.sqrt(HEAD_DIM)
    return jnp.asarray(np.concatenate([c, s], axis=1), F32).astype(BF16)


def _position_dft(n):
    r = int(round(math.sqrt(n)))
    if n <= 256 or r * r != n:
        t = np.arange(n)
        ang = 2.0 * np.pi * ((t[:, None] * t[None, :]) % n) / n
        c = jnp.asarray(np.cos(ang) / math.sqrt(n), F32)
        s = jnp.asarray(-np.sin(ang) / math.sqrt(n), F32)
        return c.astype(BF16), s.astype(BF16)
    i = np.arange(r)
    a = 2.0 * np.pi * ((i[:, None] * i[None, :]) % r) / r
    bb = 2.0 * np.pi * (((i[:, None, None] * i[None, :, None] * r) + i[:, None, None] * i[None, None, :]) % n) / n
    ca = jnp.asarray(np.cos(a), F32)[:, None, None, :]
    sa = jnp.asarray(np.sin(a), F32)[:, None, None, :]
    cb = jnp.asarray(np.cos(bb) / math.sqrt(n), F32)[None]
    sb = jnp.asarray(np.sin(bb) / math.sqrt(n), F32)[None]
    c = (ca * cb - sa * sb).reshape(n, n)
    s = (-(sa * cb + ca * sb)).reshape(n, n)
    return c.astype(BF16), s.astype(BF16)


def _dft_kernel(c_ref, s_ref, g_ref, o_ref, acc_ref):
    k = pl.program_id(1)

    @pl.when(k == 0)
    def _():
        acc_ref[...] = jnp.zeros_like(acc_ref)

    c = c_ref[...]
    s = s_ref[...]
    for i in range(g_ref.shape[0]):
        g = g_ref[i]
        acc_ref[i] += _dot(c, g[:, :GROUP_WIDTH]) + _dot(s, g[:, GROUP_WIDTH:])

    @pl.when(k == pl.num_programs(1) - 1)
    def _():
        o_ref[...] = acc_ref[...].astype(BF16)


def _position_dft_apply(g, cmat, smat, tm=1024, tk=512):
    b, n, _ = g.shape
    tm, tk = min(tm, n), min(tk, n)
    gw = GROUP_WIDTH
    return pl.pallas_call(
        _dft_kernel,
        grid=(n // tm, n // tk),
        in_specs=[pl.BlockSpec((tm, tk), lambda i, k: (i, k)),
                  pl.BlockSpec((tm, tk), lambda i, k: (i, k)),
                  pl.BlockSpec((b, tk, 2 * gw), lambda i, k: (0, k, 0))],
        out_specs=pl.BlockSpec((b, tm, gw), lambda i, k: (0, i, 0)),
        out_shape=jax.ShapeDtypeStruct((b, n, gw), BF16),
        scratch_shapes=[pltpu.VMEM((b, tm, gw), F32)],
        compiler_params=_cparams(("arbitrary", "arbitrary")),
        name="position_dft",
    )(cmat, smat, g)


def _route(logits_t, rb_ref):
    m = logits_t.shape[1]
    score, biased = [], []
    for e in range(N_EXPERTS):
        le = logits_t[e:e + 1, :]
        se = 1.0 / (1.0 + jnp.exp(-le))
        score.append(se)
        biased.append(se + rb_ref[e:e + 1, :])
    npg = EXPERTS_PER_GROUP
    grp_score = []
    for g in range(N_EXPERT_GROUPS):
        bg = biased[g * npg:(g + 1) * npg]
        pairs = [bg[i] + bg[j] for i in range(npg) for j in range(i + 1, npg)]
        grp_score.append(functools.reduce(jnp.maximum, pairs))
    best = jnp.zeros((1, m), jnp.int32)
    cur = grp_score[0]
    for g in range(1, N_EXPERT_GROUPS):
        upd = grp_score[g] > cur
        best = jnp.where(upd, g, best)
        cur = jnp.where(upd, grp_score[g], cur)
    vb, vu = [], []
    for i in range(npg):
        b_i, u_i = biased[i], score[i]
        for g in range(1, N_EXPERT_GROUPS):
            sel = best == g
            b_i = jnp.where(sel, biased[g * npg + i], b_i)
            u_i = jnp.where(sel, score[g * npg + i], u_i)
        vb.append(b_i)
        vu.append(u_i)
    i1 = jnp.zeros((1, m), jnp.int32)
    c1, u1 = vb[0], vu[0]
    for i in range(1, npg):
        upd = vb[i] > c1
        i1 = jnp.where(upd, i, i1)
        c1 = jnp.where(upd, vb[i], c1)
        u1 = jnp.where(upd, vu[i], u1)
    i2 = jnp.full((1, m), -1, jnp.int32)
    c2 = jnp.full((1, m), -jnp.inf, F32)
    u2 = jnp.zeros((1, m), F32)
    for i in range(npg):
        upd = (i1 != i) & (vb[i] > c2)
        i2 = jnp.where(upd, i, i2)
        c2 = jnp.where(upd, vb[i], c2)
        u2 = jnp.where(upd, vu[i], u2)
    tot = u1 + u2
    w1, w2 = u1 / tot, u2 / tot
    lo = jnp.minimum(i1, i2)
    hi = jnp.maximum(i1, i2)
    pair = jnp.where(lo == 0, 0, jnp.where(lo == 1, 3, 5)) + (hi - lo - 1)
    first_is_lo = i1 < i2
    return best * PAIRS_PER_GROUP + pair, jnp.where(first_is_lo, w1, w2), jnp.where(first_is_lo, w2, w1)


def _wout_kernel(ycp_ref, ya_ref, yf_ref, x_ref, g1_ref, w_ref, gn_ref, sc_ref, sh_ref,
                 rwa_ref, rwb_ref, rb_ref, tri_ref, cin_ref,
                 x1_ref, h2_ref, cls_ref, rank_ref, cnt_ref, carry_ref):
    gw = GROUP_WIDTH
    tm = x_ref.shape[1]

    @pl.when((pl.program_id(0) == 0) & (pl.program_id(1) == 0))
    def _():
        carry_ref[...] = cin_ref[...]

    mix = (_dot(ycp_ref[0], w_ref[0:2 * gw, :]) + _dot(ya_ref[0], w_ref[2 * gw:3 * gw, :])
           + _dot(yf_ref[0], w_ref[3 * gw:4 * gw, :]))
    x1 = x_ref[0] + g1_ref[0] * mix
    x1_ref[0] = x1
    h2 = _modulated_norm(x1, gn_ref[...], sc_ref[0], sh_ref[0])
    hh, hl = _split(h2)
    la = _dot(hh, rwa_ref[...]).T
    lb = _dot(hl, rwb_ref[...]).T
    logits_t = la[0:N_EXPERTS] + la[N_EXPERTS:2 * N_EXPERTS] + lb[0:N_EXPERTS]
    cls_t, ga_t, gb_t = _route(logits_t, rb_ref)
    gates = jnp.concatenate([ga_t, gb_t, jnp.zeros((LANES - 2, tm), F32)], axis=0).T
    h2_ref[0] = jnp.concatenate([h2, gates], axis=1)
    onehot = lax.broadcasted_iota(jnp.int32, (CLS_PAD, tm), 0) == cls_t
    onehot_b = jnp.where(onehot, 1.0, 0.0).astype(BF16)
    prefix = _dot(onehot_b, tri_ref[...])
    carry = carry_ref[...]
    rank = jnp.sum(jnp.where(onehot, prefix + carry[:, 0:1], 0.0), axis=0, keepdims=True)
    carry = carry + jnp.sum(onehot_b.astype(F32), axis=1, keepdims=True)
    carry_ref[...] = carry
    cnt_ref[...] = carry
    cls_ref[0] = cls_t
    rank_ref[0] = rank.astype(jnp.int32)


def _output_projection(ycp, ya, yf, x, g1, w_out, gn, sc, sh, rwa, rwb, rb, counts_in, tm=512):
    b, n, d = x.shape
    tm = min(tm, n)
    gw = GROUP_WIDTH
    tri = jnp.asarray(np.triu(np.ones((tm, tm), np.float32), 1), BF16)
    tokb = lambda width: pl.BlockSpec((1, tm, width), lambda i, j: (i, j, 0))
    vec = pl.BlockSpec((1, 1, d), lambda i, j: (i, 0, 0))
    full = lambda shape: pl.BlockSpec(shape, lambda i, j: (0, 0))
    lane_tok = pl.BlockSpec((1, 1, tm), lambda i, j: (i, 0, j))
    return pl.pallas_call(
        _wout_kernel,
        grid=(b, n // tm),
        in_specs=[tokb(2 * gw), tokb(gw), tokb(gw), tokb(d), vec, full((4 * gw, d)), full((1, d)), vec, vec,
                  full((d, LANES)), full((d, LANES)), full((N_EXPERTS, 1)), full((tm, tm)), full((CLS_PAD, LANES))],
        out_specs=[tokb(d), tokb(d + LANES), lane_tok, lane_tok, full((CLS_PAD, LANES))],
        out_shape=[jax.ShapeDtypeStruct((b, n, d), F32), jax.ShapeDtypeStruct((b, n, d + LANES), F32),
                   jax.ShapeDtypeStruct((b, 1, n), jnp.int32), jax.ShapeDtypeStruct((b, 1, n), jnp.int32),
                   jax.ShapeDtypeStruct((CLS_PAD, LANES), F32)],
        scratch_shapes=[pltpu.VMEM((CLS_PAD, LANES), F32)],
        compiler_params=_cparams(("arbitrary", "arbitrary")),
        name="out_proj_router",
    )(ycp, ya, yf, x, g1, w_out, gn.reshape(1, d), sc, sh, rwa, rwb, rb, tri, counts_in)


def _moe_layout(counts, cls_list, rank_list, tm, n_tiles):
    cnt = counts[:N_CLASSES, 0].astype(jnp.int32)
    padded = (cnt + tm - 1) // tm * tm
    off_end = jnp.cumsum(padded)
    off = off_end - padded
    pos_list = []
    for cls, rank in zip(cls_list, rank_list):
        base = jnp.zeros_like(cls)
        for c in range(N_CLASSES):
            base = jnp.where(cls == c, off[c], base)
        pos_list.append((base + rank).reshape(-1))
    tile_start = jnp.arange(n_tiles, dtype=jnp.int32) * tm
    tile_cls = jnp.sum((off_end[None, :] <= tile_start[:, None]).astype(jnp.int32), axis=1)
    valid = (tile_cls < N_CLASSES).astype(jnp.int32)
    tile_idx = jnp.minimum(jnp.arange(n_tiles, dtype=jnp.int32), off_end[-1] // tm - 1)
    tile_cls = jnp.take(tile_cls, tile_idx)
    ea = jnp.take(jnp.asarray(CLASS_EA, jnp.int32), tile_cls)
    eb = jnp.take(jnp.asarray(CLASS_EB, jnp.int32), tile_cls)
    used_tiles = off_end[-1:] // tm
    pads = jnp.concatenate([off + cnt, padded - cnt, used_tiles, n_tiles - used_tiles])
    return pos_list, pads, tile_idx, ea, eb, valid


def _row_copy(src_ref, s, dst_ref, t, sem):
    return pltpu.make_async_copy(src_ref.at[pl.ds(s, 1)], dst_ref.at[pl.ds(t, 1)], sem)


def _copy_rows(n, chunk, start_row, wait_chunk):
    def body(ci, carry):
        def issue(k, c):
            start_row(ci * chunk + k)
            return c
        lax.fori_loop(0, chunk, issue, 0, unroll=8)

        @pl.when(ci > 0)
        def _():
            wait_chunk()
        return carry
    lax.fori_loop(0, n // chunk, body, 0)
    wait_chunk()


def _dispatch_kernel(*refs, n_src, tm):
    pos_refs = refs[:n_src]
    pad_ref = refs[n_src]
    src_refs = refs[n_src + 1:2 * n_src + 1]
    xs_ref, zero_ref, sem, zsem = refs[2 * n_src + 1:]
    zero_ref[...] = jnp.zeros_like(zero_ref)

    def zero_copy(row):
        return pltpu.make_async_copy(zero_ref.at[pl.ds(0, 1)], xs_ref.at[pl.ds(row, 1)], zsem)

    def zero_tile(tile):
        return pltpu.make_async_copy(zero_ref, xs_ref.at[pl.ds(pl.multiple_of(tile * tm, tm), tm)], zsem)

    def pad_class(c, carry):
        lax.fori_loop(0, pad_ref[N_CLASSES + c], lambda i, k: (zero_copy(pad_ref[c] + i).start(), k)[1], 0)
        return carry
    lax.fori_loop(0, N_CLASSES, pad_class, 0)
    first_unused, n_unused = pad_ref[2 * N_CLASSES], pad_ref[2 * N_CLASSES + 1]
    lax.fori_loop(0, n_unused, lambda i, k: (zero_tile(first_unused + i).start(), k)[1], 0)

    for src_ref, pos_ref in zip(src_refs, pos_refs):
        n = src_ref.shape[0]
        chunk = min(ROW_CHUNK, n)
        _copy_rows(n, chunk,
                   lambda t: _row_copy(src_ref, t, xs_ref, pos_ref[t], sem).start(),
                   lambda: pltpu.make_async_copy(src_ref.at[pl.ds(0, chunk)], xs_ref.at[pl.ds(0, chunk)], sem).wait())

    def pad_wait(c, carry):
        lax.fori_loop(0, pad_ref[N_CLASSES + c], lambda i, k: (zero_copy(0).wait(), k)[1], 0)
        return carry
    lax.fori_loop(0, N_CLASSES, pad_wait, 0)
    lax.fori_loop(0, n_unused, lambda i, k: (zero_tile(0).wait(), k)[1], 0)


def _dispatch(pos_list, pads, src_list, n_slots, tm):
    n_src = len(src_list)
    width = src_list[0].shape[1]
    smem = pl.BlockSpec(memory_space=pltpu.SMEM)
    hbm = pl.BlockSpec(memory_space=pl.ANY)
    return pl.pallas_call(
        functools.partial(_dispatch_kernel, n_src=n_src, tm=tm),
        in_specs=[smem] * (n_src + 1) + [hbm] * n_src,
        out_specs=hbm,
        out_shape=jax.ShapeDtypeStruct((n_slots, width), F32),
        scratch_shapes=[pltpu.VMEM((tm, width), F32), pltpu.SemaphoreType.DMA, pltpu.SemaphoreType.DMA],
        name="moe_dispatch",
    )(*pos_list, pads, *src_list)


def _collect_kernel(*refs, n_dst):
    pos_refs = refs[:n_dst]
    ys_ref = refs[n_dst]
    dst_refs = refs[n_dst + 1:2 * n_dst + 1]
    sem = refs[2 * n_dst + 1]
    for dst_ref, pos_ref in zip(dst_refs, pos_refs):
        n = dst_ref.shape[0]
        chunk = min(ROW_CHUNK, n)
        _copy_rows(n, chunk,
                   lambda t: _row_copy(ys_ref, pos_ref[t], dst_ref, t, sem).start(),
                   lambda: pltpu.make_async_copy(ys_ref.at[pl.ds(0, chunk)], dst_ref.at[pl.ds(0, chunk)], sem).wait())


def _collect(pos_list, ys):
    n_dst = len(pos_list)
    smem = pl.BlockSpec(memory_space=pltpu.SMEM)
    hbm = pl.BlockSpec(memory_space=pl.ANY)
    return pl.pallas_call(
        functools.partial(_collect_kernel, n_dst=n_dst),
        in_specs=[smem] * n_dst + [hbm],
        out_specs=[hbm] * n_dst,
        out_shape=[jax.ShapeDtypeStruct((p.shape[0], ys.shape[1]), F32) for p in pos_list],
        scratch_shapes=[pltpu.SemaphoreType.DMA],
        name="moe_collect",
    )(*pos_list, ys)


def _moe_kernel(idx_ref, ea_ref, eb_ref, valid_ref, x_ref, w1a_ref, w3a_ref, w2a_ref, w1b_ref, w3b_ref, w2b_ref,
                o_ref):
    d = o_ref.shape[1]

    @pl.when(valid_ref[pl.program_id(0)] != 0)
    def _():
        row = x_ref[...]
        h = row[:, :d].astype(BF16)

        def expert(w1_ref, w3_ref, w2_ref):
            a = _dot(h, w1_ref[0])
            he = (a / (1.0 + jnp.exp(-a))) * _dot(h, w3_ref[0])
            return _dot(he.astype(BF16), w2_ref[0])

        o_ref[...] = (row[:, d:d + 1] * expert(w1a_ref, w3a_ref, w2a_ref)
                      + row[:, d + 1:d + 2] * expert(w1b_ref, w3b_ref, w2b_ref))

    @pl.when(valid_ref[pl.program_id(0)] == 0)
    def _():
        o_ref[...] = jnp.zeros_like(o_ref)


def _moe_grouped(xs, tile_idx, ea, eb, valid, w1, w3, w2, tm):
    n_slots, width = xs.shape
    _, d, hid = w1.shape
    wa = lambda shape: pl.BlockSpec(shape, lambda i, idx, ea, eb, v: (ea[i], 0, 0))
    wb = lambda shape: pl.BlockSpec(shape, lambda i, idx, ea, eb, v: (eb[i], 0, 0))
    return pl.pallas_call(
        _moe_kernel,
        grid_spec=pltpu.PrefetchScalarGridSpec(
            num_scalar_prefetch=4,
            grid=(n_slots // tm,),
            in_specs=[pl.BlockSpec((tm, width), lambda i, idx, ea, eb, v: (idx[i], 0)),
                      wa((1, d, hid)), wa((1, d, hid)), wa((1, hid, d)),
                      wb((1, d, hid)), wb((1, d, hid)), wb((1, hid, d))],
            out_specs=pl.BlockSpec((tm, d), lambda i, idx, ea, eb, v: (i, 0)),
        ),
        out_shape=jax.ShapeDtypeStruct((n_slots, d), F32),
        compiler_params=_cparams(("arbitrary",)),
        name="moe_grouped",
    )(tile_idx, ea, eb, valid, xs, w1, w3, w2, w1, w3, w2)


def _moe(rows_list, cls_list, rank_list, counts, w1, w3, w2, tm=256):
    rows_list = [r.reshape(-1, r.shape[-1]) for r in rows_list]
    total = sum(r.shape[0] for r in rows_list)
    n_tiles = -(-total // tm) + N_CLASSES
    pos_list, pads, tile_idx, ea, eb, valid = _moe_layout(counts, cls_list, rank_list, tm, n_tiles)
    xs = _dispatch(pos_list, pads, rows_list, n_tiles * tm, tm)
    ys = _moe_grouped(xs, tile_idx, ea, eb, valid, w1, w3, w2, tm)
    return _collect(pos_list, ys)


def _final_kernel(x_ref, f_ref, g2_ref, gn_ref, o_ref):
    x = x_ref[0] + g2_ref[0] * f_ref[0]
    ms = jnp.mean(x * x, axis=-1, keepdims=True)
    o_ref[0] = x * lax.rsqrt(ms + EPS) * gn_ref[...]


def _final_norm(x1, f, g2, gn, tm=1024):
    b, n, d = x1.shape
    tm = min(tm, n)
    tok = pl.BlockSpec((1, tm, d), lambda i, j: (i, j, 0))
    return pl.pallas_call(
        _final_kernel,
        grid=(b, n // tm),
        in_specs=[tok, tok, pl.BlockSpec((1, 1, d), lambda i, j: (i, 0, 0)), pl.BlockSpec((1, d), lambda i, j: (0, 0))],
        out_specs=tok,
        out_shape=jax.ShapeDtypeStruct((b, n, d), F32),
        compiler_params=_cparams(("arbitrary", "arbitrary")),
        name="final_norm",
    )(x1, f, g2, gn.reshape(1, d))


def _block_diag(blocks):
    g, c, _ = blocks.shape
    out = jnp.zeros((g * c, g * c), blocks.dtype)
    for i in range(g):
        out = lax.dynamic_update_slice(out, blocks[i], (i * c, i * c))
    return out


def kernel(x, c, ctx, c_ctx, ada_w, ada_b, norm1_g, norm2_g, w_in, conv_w, pool_w, pool_scale, na_rpb, w_out,
           router_w, router_b, moe_w1, moe_w3, moe_w2, final_g):
    b, n, d = x.shape
    l = ctx.shape[1]
    depth = ada_w.shape[0]
    gw = GROUP_WIDTH

    rows = -(-(b + 1) // 8) * 8
    c_all = jnp.concatenate([c, c_ctx[None, :], jnp.zeros((rows - b - 1, d), F32)], axis=0)
    mod = _modulation(c_all, ada_w, ada_b)

    def lat_mod(layer, i):
        return mod[layer, :b, i * d:(i + 1) * d].reshape(b, 1, d)

    def ctx_mod(layer, i):
        return jnp.broadcast_to(mod[layer, b, i * d:(i + 1) * d].reshape(1, 1, d), (b, 1, d))

    wd = _channel_dft()
    cn, sn = _position_dft(n)
    cl, sl = _position_dft(l)

    rw_hi, rw_lo = _split(router_w.astype(F32))
    zpad = lambda k: jnp.zeros((d, LANES - k * N_EXPERTS), BF16)
    rwa = jnp.concatenate([rw_hi, rw_lo, zpad(2)], axis=1)
    rwb = jnp.concatenate([rw_hi, zpad(1)], axis=1)
    rb = router_b.astype(F32).reshape(N_EXPERTS, 1)

    xc = ctx
    resid = None
    resid_c = None
    for layer in range(depth):
        last = layer == depth - 1
        w_in_l = w_in[layer].astype(BF16)
        w_out_l = w_out[layer].astype(BF16)
        w1 = moe_w1[layer].astype(BF16)
        w3 = moe_w3[layer].astype(BF16)
        w2 = moe_w2[layer].astype(BF16)
        pool_bd = _block_diag(pool_w[layer].astype(BF16))

        outs = _input_projection(x, norm1_g[layer], lat_mod(layer, 1), lat_mod(layer, 0), w_in_l, wd=wd, resid=resid)
        u, g = outs[0], outs[1]
        if resid is not None:
            x = outs[2]
        if last:
            outs_c = _input_projection(xc, norm1_g[layer], ctx_mod(layer, 1), ctx_mod(layer, 0),
                                       w_in_l[:, 5 * gw:7 * gw], resid=resid_c)
            uc, kc_col = outs_c[0], 0
        else:
            outs_c = _input_projection(xc, norm1_g[layer], ctx_mod(layer, 1), ctx_mod(layer, 0), w_in_l, wd=wd,
                                       resid=resid_c)
            uc, gc, kc_col = outs_c[0], outs_c[1], 5
            if resid_c is not None:
                xc = outs_c[2]

        ycp = _conv_pool(u, conv_w[layer], pool_bd, pool_scale[layer])
        ya = _neighborhood_attention(u, uc, kc_col, na_rpb[layer])
        yf = _position_dft_apply(g, cn, sn)
        x1, rows_l, cls_l, rank_l, counts = _output_projection(
            ycp, ya, yf, x, lat_mod(layer, 2), w_out_l, norm2_g[layer], lat_mod(layer, 4), lat_mod(layer, 3),
            rwa, rwb, rb, jnp.zeros((CLS_PAD, LANES), F32))
        x = x1
        if last:
            (f,) = _moe([rows_l], [cls_l], [rank_l], counts, w1, w3, w2)
        else:
            ycp_c = _conv_pool(uc, conv_w[layer], pool_bd, pool_scale[layer])
            ya_c = _context_attention(uc)
            yf_c = _position_dft_apply(gc, cl, sl)
            xc, rows_c, cls_c, rank_c, counts = _output_projection(
                ycp_c, ya_c, yf_c, xc, ctx_mod(layer, 2), w_out_l, norm2_g[layer], ctx_mod(layer, 4),
                ctx_mod(layer, 3), rwa, rwb, rb, counts)
            f, fc = _moe([rows_l, rows_c], [cls_l, cls_c], [rank_l, rank_c], counts, w1, w3, w2)
            resid_c = (fc.reshape(b, l, d), ctx_mod(layer, 5))
        resid = (f.reshape(b, n, d), lat_mod(layer, 5))
    return _final_norm(x, resid[0], resid[1], final_g)
```

```python
import functools
import math

import numpy as np
import jax
import jax.numpy as jnp
from jax import lax
from jax.experimental import pallas as pl
from jax.experimental.pallas import tpu as pltpu

F32 = jnp.float32
BF16 = jnp.bfloat16

GRID_W = 64
GROUP_WIDTH = 256
HEAD_DIM = 64
N_HEADS = GROUP_WIDTH // HEAD_DIM
CONV_K = 3
POOL_WINDOWS = (2, 4, 8, 16)
POOL_GROUP = GROUP_WIDTH // len(POOL_WINDOWS)
WIN_ROWS = 8
WIN_COLS = 16
N_EXPERTS = 16
EXPERTS_PER_GROUP = 4
N_EXPERT_GROUPS = N_EXPERTS // EXPERTS_PER_GROUP
EPS = 1e-6
NEG = -1e30

HALO = 16
NA_ROWS = 8
NA_BAND_BLOCKS = 4
NA_KBLK = 4 * GRID_W
LANES = 128

_PAIRS = [(i, j) for i in range(EXPERTS_PER_GROUP) for j in range(i + 1, EXPERTS_PER_GROUP)]
PAIRS_PER_GROUP = len(_PAIRS)
N_CLASSES = N_EXPERT_GROUPS * PAIRS_PER_GROUP
CLS_PAD = 32
CLASS_EA = [g * EXPERTS_PER_GROUP + p[0] for g in range(N_EXPERT_GROUPS) for p in _PAIRS]
CLASS_EB = [g * EXPERTS_PER_GROUP + p[1] for g in range(N_EXPERT_GROUPS) for p in _PAIRS]
CHUNK = 8
TILE_CHUNKS = 32
DUMP_CHUNKS = N_CLASSES * (TILE_CHUNKS - 1)


def _sorted_rows(tm):
    return -(-(tm + N_CLASSES * (CHUNK - 1)) // 64) * 64

VMEM_LIMIT = 56 * 1024 * 1024


def _cparams(sem):
    return pltpu.CompilerParams(dimension_semantics=sem, vmem_limit_bytes=VMEM_LIMIT)


def _split(x):
    hi = x.astype(BF16)
    lo = (x - hi.astype(F32)).astype(BF16)
    return hi, lo


def _dot(a, b):
    return jnp.dot(a, b, preferred_element_type=F32)


def _dot_nt(a, b):
    return lax.dot_general(a, b, (((1,), (1,)), ((), ())), preferred_element_type=F32)


def _dot3(a, b):
    ah, al = _split(a)
    bh, bl = _split(b)
    return _dot(ah, bh) + _dot(al, bh) + _dot(ah, bl)


def _mod_kernel(c_ref, w_ref, b_ref, o_ref):
    c = c_ref[...]
    act = c / (1.0 + jnp.exp(-c))
    o_ref[0] = _dot3(act, w_ref[0]) + b_ref[0]


def _modulation(c_all, ada_w, ada_b):
    depth, d, n6 = ada_w.shape
    rows = c_all.shape[0]
    tn = 1536
    return pl.pallas_call(
        _mod_kernel,
        grid=(depth, n6 // tn),
        in_specs=[
            pl.BlockSpec((rows, d), lambda l, j: (0, 0)),
            pl.BlockSpec((1, d, tn), lambda l, j: (l, 0, j)),
            pl.BlockSpec((1, 1, tn), lambda l, j: (l, 0, j)),
        ],
        out_specs=pl.BlockSpec((1, rows, tn), lambda l, j: (l, 0, j)),
        out_shape=jax.ShapeDtypeStruct((depth, rows, n6), F32),
        compiler_params=_cparams(("arbitrary", "arbitrary")),
        name="adaln_mod",
    )(c_all, ada_w, ada_b.reshape(depth, 1, n6))


def _modulated_norm(x, gn, sc, sh):
    ms = jnp.mean(x * x, axis=-1, keepdims=True)
    return x * lax.rsqrt(ms + EPS) * (gn * (1.0 + sc)) + sh


def _unsort(ys_ref, meta_ref):
    ys = ys_ref[...]
    pos = meta_ref[0][:, 6:7].astype(jnp.int32)
    pick = lax.broadcasted_iota(jnp.int32, (1, ys.shape[0]), 1) == pos
    pick = jnp.where(pick, 1.0, 0.0).astype(BF16)
    hi, lo = _split(ys)
    return _dot(pick, hi) + _dot(pick, lo)


def _win_kernel(*refs, n_main, has_dft, has_resid):
    it = iter(refs)
    x_ref = next(it)
    if has_resid:
        ys_ref = next(it)
        meta_ref = next(it)
        g2_ref = next(it)
    gn_ref, sc_ref, sh_ref, w_ref = next(it), next(it), next(it), next(it)
    if has_dft:
        wd_ref = next(it)
    u_ref = next(it)
    if has_dft:
        g_ref = next(it)
    if has_resid:
        xo_ref = next(it)

    x = x_ref[0]
    if has_resid:
        x = x + g2_ref[0] * _unsort(ys_ref, meta_ref)
        xo_ref[0] = x
    h = _modulated_norm(x, gn_ref[...], sc_ref[0], sh_ref[0]).astype(BF16)
    u = _dot(h, w_ref[...])
    u_ref[0] = u[:, :n_main].astype(BF16)
    if has_dft:
        g_ref[0] = _dot(u[:, n_main:].astype(BF16), wd_ref[...]).astype(BF16)


def _input_projection(x, gn, sc, sh, w, wd=None, resid=None, tm=512):
    b, n, d = x.shape
    tm = min(tm, n)
    ncols = w.shape[1]
    has_dft = wd is not None
    has_resid = resid is not None
    n_main = ncols - GROUP_WIDTH if has_dft else ncols
    tok = pl.BlockSpec((1, tm, d), lambda i, j: (i, j, 0))
    vec = pl.BlockSpec((1, 1, d), lambda i, j: (i, 0, 0))
    args, specs = [x], [tok]
    if has_resid:
        ys, meta, g2 = resid
        args += [ys, meta, g2]
        specs += [pl.BlockSpec((_sorted_rows(tm), d), lambda i, j: (i * (n // tm) + j, 0)),
                  pl.BlockSpec((1, tm, LANES), lambda i, j: (i, j, 0)), vec]
    args += [gn.reshape(1, d), sc, sh, w]
    specs += [pl.BlockSpec((1, d), lambda i, j: (0, 0)), vec, vec,
              pl.BlockSpec((d, ncols), lambda i, j: (0, 0))]
    if has_dft:
        args.append(wd)
        specs.append(pl.BlockSpec(wd.shape, lambda i, j: (0, 0)))
    out_shape = [jax.ShapeDtypeStruct((b, n, n_main), BF16)]
    out_specs = [pl.BlockSpec((1, tm, n_main), lambda i, j: (i, j, 0))]
    if has_dft:
        out_shape.append(jax.ShapeDtypeStruct((b, n, 2 * GROUP_WIDTH), BF16))
        out_specs.append(pl.BlockSpec((1, tm, 2 * GROUP_WIDTH), lambda i, j: (i, j, 0)))
    if has_resid:
        out_shape.append(jax.ShapeDtypeStruct((b, n, d), F32))
        out_specs.append(tok)
    return pl.pallas_call(
        functools.partial(_win_kernel, n_main=n_main, has_dft=has_dft, has_resid=has_resid),
        grid=(b, n // tm),
        in_specs=specs,
        out_specs=out_specs,
        out_shape=out_shape,
        compiler_params=_cparams(("arbitrary", "arbitrary")),
        name="norm_in_proj",
    )(*args)


def _convpool_kernel(av_ref, ab_ref, ac_ref, p_ref, avp_ref, acp_ref, pp_ref, avn_ref, acn_ref, pn_ref,
                     cw_ref, pw_ref, ps_ref, o_ref, *, rows, seq):
    j = pl.program_id(1)
    first = j == 0
    last = j == pl.num_programs(1) - 1
    ext_rows = rows + 2 * HALO

    def ext(main_ref, prev_ref, next_ref):
        prev = jnp.where(first, 0.0, prev_ref[0].astype(F32))
        nxt = jnp.where(last, 0.0, next_ref[0].astype(F32))
        return jnp.concatenate([prev, main_ref[0].astype(F32), nxt], axis=0)

    def shifted(a, s):
        return a if s % ext_rows == 0 else pltpu.roll(a, s % ext_rows, axis=0)

    def centre(a):
        return a[HALO:HALO + rows]

    z = ext(ac_ref, acp_ref, acn_ref) * ext(av_ref, avp_ref, avn_ref)
    cw = cw_ref[...]
    conv = cw[0:1] * shifted(z, 1) + cw[1:2] * z + cw[2:3] * shifted(z, -1)
    y_conv = ab_ref[0].astype(F32) * centre(conv)

    e = ext(p_ref, pp_ref, pn_ref)
    run = e
    sums = {}
    width = 1
    while width < max(POOL_WINDOWS):
        run = run + shifted(run, width)
        width *= 2
        sums[width] = run
    lane_grp = lax.broadcasted_iota(jnp.int32, (1, GROUP_WIDTH), 1) // POOL_GROUP
    t = j * rows + lax.broadcasted_iota(jnp.int32, (rows, 1), 0)
    win = None
    cnt = None
    for gi, w in enumerate(POOL_WINDOWS):
        ws = centre(shifted(sums[w], -(w // 2 - 1)))
        c = (jnp.minimum(t + w // 2, seq) - jnp.maximum(t - w // 2, 0)).astype(F32)
        if win is None:
            win, cnt = ws, jnp.broadcast_to(c, ws.shape)
        else:
            sel = lane_grp == gi
            win = jnp.where(sel, ws, win)
            cnt = jnp.where(sel, c, cnt)
    dlt = win / cnt - centre(e)
    y_pool = _dot(dlt.astype(BF16), pw_ref[...]) * ps_ref[...]
    o_ref[0] = jnp.concatenate([y_conv, y_pool], axis=1).astype(BF16)


def _conv_pool(u, conv_w, pool_bd, pool_scale, rows=512):
    b, n, _ = u.shape
    rows = min(rows, n)
    nh = n // HALO
    rb = rows // HALO
    gw = GROUP_WIDTH

    def main(col):
        return pl.BlockSpec((1, rows, gw), lambda i, j: (i, j, col))

    def prev(col):
        return pl.BlockSpec((1, HALO, gw), lambda i, j: (i, jnp.maximum(j * rb - 1, 0), col))

    def nxt(col):
        return pl.BlockSpec((1, HALO, gw), lambda i, j: (i, jnp.minimum((j + 1) * rb, nh - 1), col))

    small = lambda shape: pl.BlockSpec(shape, lambda i, j: (0, 0))
    return pl.pallas_call(
        functools.partial(_convpool_kernel, rows=rows, seq=n),
        grid=(b, n // rows),
        in_specs=[main(0), main(1), main(2), main(3),
                  prev(0), prev(2), prev(3), nxt(0), nxt(2), nxt(3),
                  small((CONV_K, gw)), small((gw, gw)), small((1, gw))],
        out_specs=pl.BlockSpec((1, rows, 2 * gw), lambda i, j: (i, j, 0)),
        out_shape=jax.ShapeDtypeStruct((b, n, 2 * gw), BF16),
        compiler_params=_cparams(("arbitrary", "arbitrary")),
        name="conv_pool",
    )(u, u, u, u, u, u, u, u, u, u, conv_w, pool_bd, pool_scale.reshape(1, gw))


def _na_geometry(grid_rows):
    n_groups = grid_rows // NA_ROWS
    band_rows = 4 * NA_BAND_BLOCKS
    kr = min(WIN_ROWS, grid_rows)
    starts, patterns = [], []
    for g in range(n_groups):
        s_blk = int(np.clip(2 * g - 1, 0, grid_rows // 4 - NA_BAND_BLOCKS))
        starts.append(s_blk)
        pat = np.full((NA_ROWS, band_rows), 2 * WIN_ROWS - 1, np.int32)
        for rl in range(NA_ROWS):
            r = g * NA_ROWS + rl
            r0 = int(np.clip(r - kr // 2, 0, grid_rows - kr))
            for jj in range(band_rows):
                a = 4 * s_blk + jj
                if r0 <= a < r0 + kr:
                    pat[rl, jj] = a - r + (WIN_ROWS - 1)
        patterns.append(pat)
    pairs = np.stack(patterns).reshape(n_groups, NA_ROWS, band_rows // 2, 2)
    slab_pairs, slab_of = np.unique(pairs.reshape(-1, 2), axis=0, return_inverse=True)
    return starts, slab_pairs, slab_of.reshape(-1).astype(np.int32)


def _na_bias_slabs(rpb, slab_pairs):
    col = np.arange(GRID_W)
    c0 = np.clip(col - WIN_COLS // 2, 0, GRID_W - WIN_COLS)
    rel = col[None, :] - col[:, None] + (WIN_COLS - 1)
    valid = (col[None, :] >= c0[:, None]) & (col[None, :] < c0[:, None] + WIN_COLS)
    rel = np.clip(rel, 0, 2 * WIN_COLS - 2)
    t1 = jnp.take(rpb.astype(F32), jnp.asarray(rel.reshape(-1)), axis=2)
    t1 = t1.reshape(N_HEADS, 2 * WIN_ROWS - 1, GRID_W, GRID_W)
    t1 = jnp.where(jnp.asarray(valid)[None, None], t1, NEG)
    t1 = jnp.concatenate([t1, jnp.full((N_HEADS, 1, GRID_W, GRID_W), NEG, F32)], axis=1)
    return jnp.concatenate([jnp.take(t1, jnp.asarray(slab_pairs[:, 0]), axis=1),
                            jnp.take(t1, jnp.asarray(slab_pairs[:, 1]), axis=1)], axis=-1)


def _attend(q, keys, vals, biases):
    qs = (q.astype(F32) * (HEAD_DIM ** -0.5)).astype(BF16)
    lane_head = lax.broadcasted_iota(jnp.int32, (1, GROUP_WIDTH), 1) // HEAD_DIM
    out = jnp.zeros((q.shape[0], GROUP_WIDTH), F32)
    for h in range(N_HEADS):
        mh = lane_head == h
        qh = jnp.where(mh, qs, jnp.zeros_like(qs))
        s = []
        for kb, bias in zip(keys, biases):
            sb = _dot_nt(qh, kb)
            s.append(sb if bias is None else sb + bias(h))
        m = functools.reduce(jnp.maximum, [jnp.max(sb, axis=-1, keepdims=True) for sb in s])
        p = [jnp.exp(sb - m) for sb in s]
        l = functools.reduce(lambda a, c: a + c, [jnp.sum(pb, axis=-1, keepdims=True) for pb in p])
        o = functools.reduce(lambda a, c: a + c, [_dot(pb.astype(BF16), vb) for pb, vb in zip(p, vals)])
        out = jnp.where(mh, o * (1.0 / l), out)
    return out


def _na_kernel(start_ref, slab_ref, *refs):
    del start_ref
    q_ref = refs[0]
    k_refs = refs[1:1 + NA_BAND_BLOCKS]
    v_refs = refs[1 + NA_BAND_BLOCKS:1 + 2 * NA_BAND_BLOCKS]
    kc_ref, vc_ref, tbl_ref, o_ref = refs[1 + 2 * NA_BAND_BLOCKS:]
    keys = [r[0] for r in k_refs] + [kc_ref[0]]
    vals = [r[0] for r in v_refs] + [vc_ref[0]]
    slabs_per_row = 2 * NA_BAND_BLOCKS
    base = pl.program_id(0) * (NA_ROWS * slabs_per_row)

    def bias(h, blk):
        rows = []
        for rl in range(NA_ROWS):
            idx = base + rl * slabs_per_row + 2 * blk
            rows.append(jnp.concatenate([tbl_ref[h, slab_ref[idx]], tbl_ref[h, slab_ref[idx + 1]]], axis=1))
        return jnp.concatenate(rows, axis=0)

    biases = [functools.partial(bias, blk=blk) for blk in range(NA_BAND_BLOCKS)]
    o_ref[0] = _attend(q_ref[0], keys, vals, biases + [None]).astype(BF16)


def _neighborhood_attention(u, uc, kc_col, rpb):
    b, n, _ = u.shape
    l = uc.shape[1]
    grid_rows = n // GRID_W
    starts, slab_pairs, slab_of = _na_geometry(grid_rows)
    tbl = _na_bias_slabs(rpb, slab_pairs)
    nq = NA_ROWS * GRID_W
    gw = GROUP_WIDTH

    def band(col, jj):
        return pl.BlockSpec((1, NA_KBLK, gw), lambda g, i, st, sl: (i, st[g] + jj, col))

    return pl.pallas_call(
        _na_kernel,
        grid_spec=pltpu.PrefetchScalarGridSpec(
            num_scalar_prefetch=2,
            grid=(grid_rows // NA_ROWS, b),
            in_specs=[pl.BlockSpec((1, nq, gw), lambda g, i, st, sl: (i, g, 4))]
            + [band(5, jj) for jj in range(NA_BAND_BLOCKS)]
            + [band(6, jj) for jj in range(NA_BAND_BLOCKS)]
            + [pl.BlockSpec((1, l, gw), lambda g, i, st, sl: (i, 0, kc_col)),
               pl.BlockSpec((1, l, gw), lambda g, i, st, sl: (i, 0, kc_col + 1)),
               pl.BlockSpec(tbl.shape, lambda g, i, st, sl: (0, 0, 0, 0))],
            out_specs=pl.BlockSpec((1, nq, gw), lambda g, i, st, sl: (i, g, 0)),
        ),
        out_shape=jax.ShapeDtypeStruct((b, n, gw), BF16),
        compiler_params=_cparams(("arbitrary", "arbitrary")),
        name="neighborhood_attn",
    )(jnp.asarray(starts, jnp.int32), jnp.asarray(slab_of), u, *([u] * (2 * NA_BAND_BLOCKS)), uc, uc, tbl)


def _ctx_attn_kernel(q_ref, k_ref, v_ref, o_ref):
    o_ref[0] = _attend(q_ref[0], [k_ref[0]], [v_ref[0]], [None]).astype(BF16)


def _context_attention(uc):
    b, l, _ = uc.shape
    gw = GROUP_WIDTH
    return pl.pallas_call(
        _ctx_attn_kernel,
        grid=(b,),
        in_specs=[pl.BlockSpec((1, l, gw), lambda i, col=col: (i, 0, col)) for col in (4, 5, 6)],
        out_specs=pl.BlockSpec((1, l, gw), lambda i: (i, 0, 0)),
        out_shape=jax.ShapeDtypeStruct((b, l, gw), BF16),
        compiler_params=_cparams(("arbitrary",)),
        name="context_attn",
    )(uc, uc, uc)


def _channel_dft():
    j = np.arange(HEAD_DIM)
    ang = 2.0 * np.pi * ((j[:, None] * j[None, :]) % HEAD_DIM) / HEAD_DIM
    eye = np.eye(N_HEADS)
    c = np.kron(eye, np.cos(ang)) / math.sqrt(HEAD_DIM)
    s = np.kron(eye, np.sin(ang)) / math.sqrt(HEAD_DIM)
    return jnp.asarray(np.concatenate([c, s], axis=1), F32).astype(BF16)


def _position_dft(n):
    r = int(round(math.sqrt(n)))
    if n <= 256 or r * r != n:
        t = np.arange(n)
        ang = 2.0 * np.pi * ((t[:, None] * t[None, :]) % n) / n
        c = jnp.asarray(np.cos(ang) / math.sqrt(n), F32)
        s = jnp.asarray(-np.sin(ang) / math.sqrt(n), F32)
        return c.astype(BF16), s.astype(BF16)
    i = np.arange(r)
    a = 2.0 * np.pi * ((i[:, None] * i[None, :]) % r) / r
    bb = 2.0 * np.pi * (((i[:, None, None] * i[None, :, None] * r) + i[:, None, None] * i[None, None, :]) % n) / n
    ca = jnp.asarray(np.cos(a), F32)[:, None, None, :]
    sa = jnp.asarray(np.sin(a), F32)[:, None, None, :]
    cb = jnp.asarray(np.cos(bb) / math.sqrt(n), F32)[None]
    sb = jnp.asarray(np.sin(bb) / math.sqrt(n), F32)[None]
    c = (ca * cb - sa * sb).reshape(n, n)
    s = (-(sa * cb + ca * sb)).reshape(n, n)
    return c.astype(BF16), s.astype(BF16)


def _dft_kernel(c_ref, s_ref, g_ref, o_ref, acc_ref):
    k = pl.program_id(1)

    @pl.when(k == 0)
    def _():
        acc_ref[...] = jnp.zeros_like(acc_ref)

    c = c_ref[...]
    s = s_ref[...]
    for i in range(g_ref.shape[0]):
        g = g_ref[i]
        acc_ref[i] += _dot(c, g[:, :GROUP_WIDTH]) + _dot(s, g[:, GROUP_WIDTH:])

    @pl.when(k == pl.num_programs(1) - 1)
    def _():
        o_ref[...] = acc_ref[...].astype(BF16)


def _position_dft_apply(g, cmat, smat, tm=1024, tk=512):
    b, n, _ = g.shape
    tm, tk = min(tm, n), min(tk, n)
    gw = GROUP_WIDTH
    return pl.pallas_call(
        _dft_kernel,
        grid=(n // tm, n // tk),
        in_specs=[pl.BlockSpec((tm, tk), lambda i, k: (i, k)),
                  pl.BlockSpec((tm, tk), lambda i, k: (i, k)),
                  pl.BlockSpec((b, tk, 2 * gw), lambda i, k: (0, k, 0))],
        out_specs=pl.BlockSpec((b, tm, gw), lambda i, k: (0, i, 0)),
        out_shape=jax.ShapeDtypeStruct((b, n, gw), BF16),
        scratch_shapes=[pltpu.VMEM((b, tm, gw), F32)],
        compiler_params=_cparams(("arbitrary", "arbitrary")),
        name="position_dft",
    )(cmat, smat, g)


def _route(logits_t, rb_ref):
    m = logits_t.shape[1]
    score, biased = [], []
    for e in range(N_EXPERTS):
        le = logits_t[e:e + 1, :]
        se = 1.0 / (1.0 + jnp.exp(-le))
        score.append(se)
        biased.append(se + rb_ref[e:e + 1, :])
    npg = EXPERTS_PER_GROUP
    grp_score = []
    for g in range(N_EXPERT_GROUPS):
        bg = biased[g * npg:(g + 1) * npg]
        pairs = [bg[i] + bg[j] for i in range(npg) for j in range(i + 1, npg)]
        grp_score.append(functools.reduce(jnp.maximum, pairs))
    best = jnp.zeros((1, m), jnp.int32)
    cur = grp_score[0]
    for g in range(1, N_EXPERT_GROUPS):
        upd = grp_score[g] > cur
        best = jnp.where(upd, g, best)
        cur = jnp.where(upd, grp_score[g], cur)
    vb, vu = [], []
    for i in range(npg):
        b_i, u_i = biased[i], score[i]
        for g in range(1, N_EXPERT_GROUPS):
            sel = best == g
            b_i = jnp.where(sel, biased[g * npg + i], b_i)
            u_i = jnp.where(sel, score[g * npg + i], u_i)
        vb.append(b_i)
        vu.append(u_i)
    i1 = jnp.zeros((1, m), jnp.int32)
    c1, u1 = vb[0], vu[0]
    for i in range(1, npg):
        upd = vb[i] > c1
        i1 = jnp.where(upd, i, i1)
        c1 = jnp.where(upd, vb[i], c1)
        u1 = jnp.where(upd, vu[i], u1)
    i2 = jnp.full((1, m), -1, jnp.int32)
    c2 = jnp.full((1, m), -jnp.inf, F32)
    u2 = jnp.zeros((1, m), F32)
    for i in range(npg):
        upd = (i1 != i) & (vb[i] > c2)
        i2 = jnp.where(upd, i, i2)
        c2 = jnp.where(upd, vb[i], c2)
        u2 = jnp.where(upd, vu[i], u2)
    tot = u1 + u2
    w1, w2 = u1 / tot, u2 / tot
    lo = jnp.minimum(i1, i2)
    hi = jnp.maximum(i1, i2)
    pair = jnp.where(lo == 0, 0, jnp.where(lo == 1, 3, 5)) + (hi - lo - 1)
    first_is_lo = i1 < i2
    return best * PAIRS_PER_GROUP + pair, jnp.where(first_is_lo, w1, w2), jnp.where(first_is_lo, w2, w1)


def _three_way(v):
    hi = v.astype(BF16).astype(F32)
    mid = (v - hi).astype(BF16).astype(F32)
    return hi, mid, v - hi - mid


def _wout_kernel(ycp_ref, ya_ref, yf_ref, x_ref, g1_ref, w_ref, gn_ref, sc_ref, sh_ref,
                 rwa_ref, rwb_ref, rb_ref, tri_ref, ltri_ref,
                 x1_ref, rows_ref, meta_ref, cnt_ref):
    gw = GROUP_WIDTH
    tm = x_ref.shape[1]
    mix = (_dot(ycp_ref[0], w_ref[0:2 * gw, :]) + _dot(ya_ref[0], w_ref[2 * gw:3 * gw, :])
           + _dot(yf_ref[0], w_ref[3 * gw:4 * gw, :]))
    x1 = x_ref[0] + g1_ref[0] * mix
    x1_ref[0] = x1
    h2 = _modulated_norm(x1, gn_ref[...], sc_ref[0], sh_ref[0])
    hh, hl = _split(h2)
    la = _dot(hh, rwa_ref[...]).T
    lb = _dot(hl, rwb_ref[...]).T
    logits_t = la[0:N_EXPERTS] + la[N_EXPERTS:2 * N_EXPERTS] + lb[0:N_EXPERTS]
    cls_t, ga_t, gb_t = _route(logits_t, rb_ref)

    onehot = lax.broadcasted_iota(jnp.int32, (CLS_PAD, tm), 0) == cls_t
    onehot_b = jnp.where(onehot, 1.0, 0.0).astype(BF16)
    prefix = _dot(onehot_b, tri_ref[...])
    cnt = jnp.sum(onehot_b.astype(F32), axis=1, keepdims=True)
    padded = jnp.floor((cnt + (CHUNK - 1)) * (1.0 / CHUNK)) * CHUNK
    seg_start = _dot(ltri_ref[...], jnp.broadcast_to(padded, (CLS_PAD, LANES)).astype(BF16))[:, 0:1]
    local_pos = jnp.sum(jnp.where(onehot, prefix + seg_start, 0.0), axis=0, keepdims=True)
    perm = (lax.broadcasted_iota(jnp.int32, (rows_ref.shape[2], tm), 0) == local_pos.astype(jnp.int32))

    side = jnp.concatenate(list(_three_way(ga_t)) + list(_three_way(gb_t)) + [local_pos,
                           jnp.zeros((LANES - 7, tm), F32)], axis=0).T
    meta_ref[0] = side
    lane = lax.broadcasted_iota(jnp.int32, (1, LANES), 1)
    rows = jnp.concatenate([h2.astype(BF16), jnp.where(lane < 6, side, 0.0).astype(BF16)], axis=1)
    rows_ref[0, 0] = _dot(jnp.where(perm, 1.0, 0.0).astype(BF16), rows)
    cnt_ref[0, 0] = _dot_nt(jnp.ones((8, tm), BF16), onehot_b)


def _output_projection(ycp, ya, yf, x, g1, w_out, gn, sc, sh, rwa, rwb, rb, tm=512):
    b, n, d = x.shape
    tm = min(tm, n)
    nt = n // tm
    gw = GROUP_WIDTH
    tri = jnp.asarray(np.triu(np.ones((tm, tm), np.float32), 1), BF16)
    ltri = jnp.asarray(np.tril(np.ones((CLS_PAD, CLS_PAD), np.float32), -1), BF16)
    tokb = lambda width: pl.BlockSpec((1, tm, width), lambda i, j: (i, j, 0))
    vec = pl.BlockSpec((1, 1, d), lambda i, j: (i, 0, 0))
    full = lambda shape: pl.BlockSpec(shape, lambda i, j: (0, 0))
    return pl.pallas_call(
        _wout_kernel,
        grid=(b, nt),
        in_specs=[tokb(2 * gw), tokb(gw), tokb(gw), tokb(d), vec, full((4 * gw, d)), full((1, d)), vec, vec,
                  full((d, LANES)), full((d, LANES)), full((N_EXPERTS, 1)), full((tm, tm)), full((CLS_PAD, CLS_PAD))],
        out_specs=[tokb(d),
                   pl.BlockSpec((1, 1, _sorted_rows(tm), d + LANES), lambda i, j: (i, j, 0, 0)),
                   tokb(LANES),
                   pl.BlockSpec((1, 1, 8, CLS_PAD), lambda i, j: (i, j, 0, 0))],
        out_shape=[jax.ShapeDtypeStruct((b, n, d), F32),
                   jax.ShapeDtypeStruct((b, nt, _sorted_rows(tm), d + LANES), F32),
                   jax.ShapeDtypeStruct((b, n, LANES), F32),
                   jax.ShapeDtypeStruct((b, nt, 8, CLS_PAD), F32)],
        compiler_params=_cparams(("arbitrary", "arbitrary")),
        name="out_proj_router",
    )(ycp, ya, yf, x, g1, w_out, gn.reshape(1, d), sc, sh, rwa, rwb, rb, tri, ltri)


def _moe_layout(cnt, chunks_per_tile, n_slots):
    n_prod = cnt.shape[0]
    n_chunks = n_prod * chunks_per_tile
    ids = lambda n: jnp.arange(n, dtype=jnp.int32)
    nch = (cnt + CHUNK - 1) // CHUNK
    seg_src = ids(n_prod)[:, None] * chunks_per_tile + jnp.cumsum(nch, axis=1) - nch
    used = jnp.sum(nch, axis=1)
    cls_chunks = jnp.sum(nch, axis=0)
    cls_padded = (cls_chunks + TILE_CHUNKS - 1) // TILE_CHUNKS * TILE_CHUNKS
    cls_end = jnp.cumsum(cls_padded)
    cls_start = cls_end - cls_padded
    pad_before = jnp.cumsum(cls_padded - cls_chunks) - (cls_padded - cls_chunks)
    seg_dst = (cls_start[None, :] + jnp.cumsum(nch, axis=0) - nch).T.reshape(-1)
    seg_n = nch.T.reshape(-1)
    seg_src = seg_src.T.reshape(-1)
    slot = ids(n_slots)
    cls_of_slot = jnp.minimum(jnp.sum((cls_end[None, :] <= slot[:, None]).astype(jnp.int32), axis=1), N_CLASSES - 1)
    by_cls = cls_of_slot[:, None] == ids(N_CLASSES)[None, :]
    cls_lookup = lambda table: jnp.sum(jnp.where(by_cls, table[None, :], 0), axis=1)
    seg_of_slot = jnp.sum((seg_dst[None, :] <= slot[:, None]).astype(jnp.int32), axis=1) - 1
    by_seg = seg_of_slot[:, None] == ids(seg_n.shape[0])[None, :]
    seg_lookup = lambda table: jnp.sum(jnp.where(by_seg, table[None, :], 0), axis=1)
    q = slot - cls_lookup(cls_start)
    real = q < cls_lookup(cls_chunks)
    chunk = seg_lookup(seg_src) + slot - seg_lookup(seg_dst)
    src = jnp.where(real, chunk, 0)
    dst = jnp.where(real, chunk, n_chunks + jnp.minimum(cls_lookup(pad_before) + q - cls_lookup(cls_chunks),
                                                        DUMP_CHUNKS - 1))
    n_active = cls_end[-1:] // TILE_CHUNKS
    tile_cls = cls_of_slot[::TILE_CHUNKS]
    by_tile = tile_cls[:, None] == ids(N_CLASSES)[None, :]
    ea = jnp.sum(jnp.where(by_tile, jnp.asarray(CLASS_EA, jnp.int32)[None, :], 0), axis=1)
    eb = jnp.sum(jnp.where(by_tile, jnp.asarray(CLASS_EB, jnp.int32)[None, :], 0), axis=1)
    n_pad = jnp.sum(cls_padded - cls_chunks, keepdims=True)
    tail_start = jnp.concatenate([ids(n_prod) * chunks_per_tile + used, n_chunks + n_pad])
    tail_len = jnp.concatenate([chunks_per_tile - used, DUMP_CHUNKS - n_pad])
    return src, dst, ea, eb, n_active, tail_start, tail_len


def _moe_kernel(src_ref, dst_ref, ea_ref, eb_ref, n_active_ref, tail_start_ref, tail_len_ref,
                xs_ref, w1a_ref, w3a_ref, w2a_ref, w1b_ref, w3b_ref, w2b_ref, ys_ref,
                xbuf, ybuf, zbuf, gsem, ssem, zsem, *, n_prod):
    del ea_ref, eb_ref
    d = ybuf.shape[2]
    tile_rows = TILE_CHUNKS * CHUNK
    j = pl.program_id(0)
    n_active = n_active_ref[0]
    slot = j % 2

    def rows(ref, chunk):
        return ref.at[pl.ds(pl.multiple_of(chunk * CHUNK, CHUNK), CHUNK)]

    def start_gather(tile, buf):
        for k in range(TILE_CHUNKS):
            pltpu.make_async_copy(rows(xs_ref, src_ref[tile * TILE_CHUNKS + k]),
                                  xbuf.at[buf, pl.ds(k * CHUNK, CHUNK)], gsem.at[buf]).start()

    def start_scatter(tile, buf):
        for k in range(TILE_CHUNKS):
            pltpu.make_async_copy(ybuf.at[buf, pl.ds(k * CHUNK, CHUNK)],
                                  rows(ys_ref, dst_ref[tile * TILE_CHUNKS + k]), ssem.at[buf]).start()

    def wait_gather(buf):
        pltpu.make_async_copy(xs_ref.at[pl.ds(0, tile_rows)], xbuf.at[buf], gsem.at[buf]).wait()

    def wait_scatter(buf):
        pltpu.make_async_copy(ybuf.at[buf], ys_ref.at[pl.ds(0, tile_rows)], ssem.at[buf]).wait()

    def per_unwritten_chunk(wait):
        def range_body(t, carry):
            def chunk_body(i, c):
                copy = pltpu.make_async_copy(zbuf, rows(ys_ref, tail_start_ref[t] + i), zsem)
                copy.wait() if wait else copy.start()
                return c
            lax.fori_loop(0, tail_len_ref[t], chunk_body, 0)
            return carry
        lax.fori_loop(0, n_prod + 1, range_body, 0)

    @pl.when(j == 0)
    def _():
        zbuf[...] = jnp.zeros_like(zbuf)
        per_unwritten_chunk(wait=False)
        start_gather(0, 0)

    @pl.when(j < n_active)
    def _():
        @pl.when(j + 1 < n_active)
        def _():
            start_gather(j + 1, 1 - slot)

        wait_gather(slot)

        @pl.when(j >= 2)
        def _():
            wait_scatter(slot)

        row = xbuf[slot]
        h = row[:, :d].astype(BF16)

        def expert(w1_ref, w3_ref, w2_ref):
            a = _dot(h, w1_ref[0])
            he = (a / (1.0 + jnp.exp(-a))) * _dot(h, w3_ref[0])
            return _dot(he.astype(BF16), w2_ref[0])

        ga = row[:, d:d + 1] + row[:, d + 1:d + 2] + row[:, d + 2:d + 3]
        gb = row[:, d + 3:d + 4] + row[:, d + 4:d + 5] + row[:, d + 5:d + 6]
        ybuf[slot] = ga * expert(w1a_ref, w3a_ref, w2a_ref) + gb * expert(w1b_ref, w3b_ref, w2b_ref)
        start_scatter(j, slot)

    @pl.when(j == n_active - 1)
    def _():
        @pl.when(j >= 1)
        def _():
            wait_scatter(1 - slot)
        wait_scatter(slot)
        per_unwritten_chunk(wait=True)


def _moe(rows, counts, w1, w3, w2):
    b, nt, sorted_rows, width = rows.shape
    _, d, hid = w1.shape
    n_prod = b * nt
    n_chunks = n_prod * sorted_rows // CHUNK
    n_tiles = -(-n_chunks // TILE_CHUNKS) + N_CLASSES
    cnt = counts[:, :, 0, :N_CLASSES].reshape(n_prod, N_CLASSES).astype(jnp.int32)
    tables = _moe_layout(cnt, sorted_rows // CHUNK, n_tiles * TILE_CHUNKS)
    hbm = pl.BlockSpec(memory_space=pl.ANY)
    n_pre = len(tables)
    wa = lambda shape: pl.BlockSpec(shape, lambda i, *pre: (pre[2][i], 0, 0))
    wb = lambda shape: pl.BlockSpec(shape, lambda i, *pre: (pre[3][i], 0, 0))
    tile_rows = TILE_CHUNKS * CHUNK
    return pl.pallas_call(
        functools.partial(_moe_kernel, n_prod=n_prod),
        grid_spec=pltpu.PrefetchScalarGridSpec(
            num_scalar_prefetch=n_pre,
            grid=(n_tiles,),
            in_specs=[hbm, wa((1, d, hid)), wa((1, d, hid)), wa((1, hid, d)),
                      wb((1, d, hid)), wb((1, d, hid)), wb((1, hid, d))],
            out_specs=hbm,
            scratch_shapes=[pltpu.VMEM((2, tile_rows, width), F32), pltpu.VMEM((2, tile_rows, d), F32),
                            pltpu.VMEM((CHUNK, d), F32), pltpu.SemaphoreType.DMA((2,)),
                            pltpu.SemaphoreType.DMA((2,)), pltpu.SemaphoreType.DMA],
        ),
        out_shape=jax.ShapeDtypeStruct(((n_chunks + DUMP_CHUNKS) * CHUNK, d), F32),
        compiler_params=_cparams(("arbitrary",)),
        name="moe_grouped",
    )(*tables, rows.reshape(n_chunks * CHUNK, width), w1, w3, w2, w1, w3, w2)


def _final_kernel(x_ref, ys_ref, meta_ref, g2_ref, gn_ref, o_ref):
    x = x_ref[0] + g2_ref[0] * _unsort(ys_ref, meta_ref)
    ms = jnp.mean(x * x, axis=-1, keepdims=True)
    o_ref[0] = x * lax.rsqrt(ms + EPS) * gn_ref[...]


def _final_norm(x1, resid, gn, tm=512):
    b, n, d = x1.shape
    ys, meta, g2 = resid
    tm = min(tm, n)
    tok = pl.BlockSpec((1, tm, d), lambda i, j: (i, j, 0))
    return pl.pallas_call(
        _final_kernel,
        grid=(b, n // tm),
        in_specs=[tok, pl.BlockSpec((_sorted_rows(tm), d), lambda i, j: (i * (n // tm) + j, 0)),
                  pl.BlockSpec((1, tm, LANES), lambda i, j: (i, j, 0)),
                  pl.BlockSpec((1, 1, d), lambda i, j: (i, 0, 0)), pl.BlockSpec((1, d), lambda i, j: (0, 0))],
        out_specs=tok,
        out_shape=jax.ShapeDtypeStruct((b, n, d), F32),
        compiler_params=_cparams(("arbitrary", "arbitrary")),
        name="final_norm",
    )(x1, ys, meta, g2, gn.reshape(1, d))


def _block_diag(blocks):
    g, c, _ = blocks.shape
    out = jnp.zeros((g * c, g * c), blocks.dtype)
    for i in range(g):
        out = lax.dynamic_update_slice(out, blocks[i], (i * c, i * c))
    return out


def kernel(x, c, ctx, c_ctx, ada_w, ada_b, norm1_g, norm2_g, w_in, conv_w, pool_w, pool_scale, na_rpb, w_out,
           router_w, router_b, moe_w1, moe_w3, moe_w2, final_g):
    b, n, d = x.shape
    l = ctx.shape[1]
    depth = ada_w.shape[0]
    gw = GROUP_WIDTH

    rows = -(-(b + 1) // 8) * 8
    c_all = jnp.concatenate([c, c_ctx[None, :], jnp.zeros((rows - b - 1, d), F32)], axis=0)
    mod = _modulation(c_all, ada_w, ada_b)

    def lat_mod(layer, i):
        return mod[layer, :b, i * d:(i + 1) * d].reshape(b, 1, d)

    def ctx_mod(layer, i):
        return jnp.broadcast_to(mod[layer, b, i * d:(i + 1) * d].reshape(1, 1, d), (b, 1, d))

    wd = _channel_dft()
    cn, sn = _position_dft(n)
    cl, sl = _position_dft(l)

    rw_hi, rw_lo = _split(router_w.astype(F32))
    zpad = lambda k: jnp.zeros((d, LANES - k * N_EXPERTS), BF16)
    rwa = jnp.concatenate([rw_hi, rw_lo, zpad(2)], axis=1)
    rwb = jnp.concatenate([rw_hi, zpad(1)], axis=1)
    rb = router_b.astype(F32).reshape(N_EXPERTS, 1)

    xc = ctx
    resid = None
    resid_c = None
    for layer in range(depth):
        last = layer == depth - 1
        w_in_l = w_in[layer].astype(BF16)
        w_out_l = w_out[layer].astype(BF16)
        w1 = moe_w1[layer].astype(BF16)
        w3 = moe_w3[layer].astype(BF16)
        w2 = moe_w2[layer].astype(BF16)
        pool_bd = _block_diag(pool_w[layer].astype(BF16))

        outs = _input_projection(x, norm1_g[layer], lat_mod(layer, 1), lat_mod(layer, 0), w_in_l, wd=wd, resid=resid)
        u, g = outs[0], outs[1]
        if resid is not None:
            x = outs[2]
        if last:
            outs_c = _input_projection(xc, norm1_g[layer], ctx_mod(layer, 1), ctx_mod(layer, 0),
                                       w_in_l[:, 5 * gw:7 * gw], resid=resid_c)
            uc, kc_col = outs_c[0], 0
        else:
            outs_c = _input_projection(xc, norm1_g[layer], ctx_mod(layer, 1), ctx_mod(layer, 0), w_in_l, wd=wd,
                                       resid=resid_c)
            uc, gc, kc_col = outs_c[0], outs_c[1], 5
            if resid_c is not None:
                xc = outs_c[2]

        ycp = _conv_pool(u, conv_w[layer], pool_bd, pool_scale[layer])
        ya = _neighborhood_attention(u, uc, kc_col, na_rpb[layer])
        yf = _position_dft_apply(g, cn, sn)
        x, rows_l, meta_l, counts_l = _output_projection(
            ycp, ya, yf, x, lat_mod(layer, 2), w_out_l, norm2_g[layer], lat_mod(layer, 4), lat_mod(layer, 3),
            rwa, rwb, rb)
        resid = (_moe(rows_l, counts_l, w1, w3, w2), meta_l, lat_mod(layer, 5))
        if not last:
            ycp_c = _conv_pool(uc, conv_w[layer], pool_bd, pool_scale[layer])
            ya_c = _context_attention(uc)
            yf_c = _position_dft_apply(gc, cl, sl)
            xc, rows_c, meta_c, counts_c = _output_projection(
                ycp_c, ya_c, yf_c, xc, ctx_mod(layer, 2), w_out_l, norm2_g[layer], ctx_mod(layer, 4),
                ctx_mod(layer, 3), rwa, rwb, rb)
            resid_c = (_moe(rows_c, counts_c, w1, w3, w2), meta_c, ctx_mod(layer, 5))
    return _final_norm(x, resid, final_g)
```

```python
import functools
import math

import numpy as np
import jax
import jax.numpy as jnp
from jax import lax
from jax.experimental import pallas as pl
from jax.experimental.pallas import tpu as pltpu

F32 = jnp.float32
BF16 = jnp.bfloat16

GRID_W = 64
GROUP_WIDTH = 256
HEAD_DIM = 64
N_HEADS = GROUP_WIDTH // HEAD_DIM
CONV_K = 3
POOL_WINDOWS = (2, 4, 8, 16)
POOL_GROUP = GROUP_WIDTH // len(POOL_WINDOWS)
WIN_ROWS = 8
WIN_COLS = 16
N_EXPERTS = 16
EXPERTS_PER_GROUP = 4
N_EXPERT_GROUPS = N_EXPERTS // EXPERTS_PER_GROUP
EPS = 1e-6
NEG = -1e30

HALO = 16
NA_ROWS = 8
NA_BAND_BLOCKS = 4
NA_KBLK = 4 * GRID_W
LANES = 128

_PAIRS = [(i, j) for i in range(EXPERTS_PER_GROUP) for j in range(i + 1, EXPERTS_PER_GROUP)]
PAIRS_PER_GROUP = len(_PAIRS)
N_CLASSES = N_EXPERT_GROUPS * PAIRS_PER_GROUP
CLS_PAD = 32
CLASS_EA = [g * EXPERTS_PER_GROUP + p[0] for g in range(N_EXPERT_GROUPS) for p in _PAIRS]
CLASS_EB = [g * EXPERTS_PER_GROUP + p[1] for g in range(N_EXPERT_GROUPS) for p in _PAIRS]
CHUNK = 8
TILE_CHUNKS = 32
DUMP_CHUNKS = N_CLASSES * (TILE_CHUNKS - 1)


def _sorted_rows(tm):
    return -(-(tm + N_CLASSES * (CHUNK - 1)) // 64) * 64

VMEM_LIMIT = 56 * 1024 * 1024


def _cparams(sem):
    return pltpu.CompilerParams(dimension_semantics=sem, vmem_limit_bytes=VMEM_LIMIT)


def _split(x):
    hi = x.astype(BF16)
    lo = (x - hi.astype(F32)).astype(BF16)
    return hi, lo


def _dot(a, b):
    return jnp.dot(a, b, preferred_element_type=F32)


def _dot_nt(a, b):
    return lax.dot_general(a, b, (((1,), (1,)), ((), ())), preferred_element_type=F32)


def _dot3(a, b):
    ah, al = _split(a)
    bh, bl = _split(b)
    return _dot(ah, bh) + _dot(al, bh) + _dot(ah, bl)


def _mod_kernel(c_ref, w_ref, b_ref, o_ref):
    c = c_ref[...]
    act = c / (1.0 + jnp.exp(-c))
    o_ref[0] = _dot3(act, w_ref[0]) + b_ref[0]


def _modulation(c_all, ada_w, ada_b):
    depth, d, n6 = ada_w.shape
    rows = c_all.shape[0]
    tn = 1536
    return pl.pallas_call(
        _mod_kernel,
        grid=(depth, n6 // tn),
        in_specs=[
            pl.BlockSpec((rows, d), lambda l, j: (0, 0)),
            pl.BlockSpec((1, d, tn), lambda l, j: (l, 0, j)),
            pl.BlockSpec((1, 1, tn), lambda l, j: (l, 0, j)),
        ],
        out_specs=pl.BlockSpec((1, rows, tn), lambda l, j: (l, 0, j)),
        out_shape=jax.ShapeDtypeStruct((depth, rows, n6), F32),
        compiler_params=_cparams(("arbitrary", "arbitrary")),
        name="adaln_mod",
    )(c_all, ada_w, ada_b.reshape(depth, 1, n6))


def _modulated_norm(x, gn, sc, sh):
    ms = jnp.mean(x * x, axis=-1, keepdims=True)
    return x * lax.rsqrt(ms + EPS) * (gn * (1.0 + sc)) + sh


def _unsort(ys_ref, meta_ref):
    ys = ys_ref[...]
    pos = meta_ref[0][:, 6:7].astype(jnp.int32)
    pick = lax.broadcasted_iota(jnp.int32, (1, ys.shape[0]), 1) == pos
    pick = jnp.where(pick, 1.0, 0.0).astype(BF16)
    hi, lo = _split(ys)
    return _dot(pick, hi) + _dot(pick, lo)


def _win_kernel(*refs, n_main, has_dft, has_resid):
    it = iter(refs)
    x_ref = next(it)
    if has_resid:
        ys_ref = next(it)
        meta_ref = next(it)
        g2_ref = next(it)
    gn_ref, sc_ref, sh_ref, w_ref = next(it), next(it), next(it), next(it)
    if has_dft:
        wd_ref = next(it)
    u_ref = next(it)
    if has_dft:
        g_ref = next(it)
    if has_resid:
        xo_ref = next(it)

    x = x_ref[0]
    if has_resid:
        x = x + g2_ref[0] * _unsort(ys_ref, meta_ref)
        xo_ref[0] = x
    h = _modulated_norm(x, gn_ref[...], sc_ref[0], sh_ref[0]).astype(BF16)
    u = _dot(h, w_ref[...])
    u_ref[0] = u[:, :n_main].astype(BF16)
    if has_dft:
        g_ref[0] = _dot(u[:, n_main:].astype(BF16), wd_ref[...]).astype(BF16)


def _input_projection(x, gn, sc, sh, w, wd=None, resid=None, tm=512):
    b, n, d = x.shape
    tm = min(tm, n)
    ncols = w.shape[1]
    has_dft = wd is not None
    has_resid = resid is not None
    n_main = ncols - GROUP_WIDTH if has_dft else ncols
    tok = pl.BlockSpec((1, tm, d), lambda i, j: (i, j, 0))
    vec = pl.BlockSpec((1, 1, d), lambda i, j: (i, 0, 0))
    args, specs = [x], [tok]
    if has_resid:
        ys, meta, g2 = resid
        args += [ys, meta, g2]
        specs += [pl.BlockSpec((_sorted_rows(tm), d), lambda i, j: (i * (n // tm) + j, 0)),
                  pl.BlockSpec((1, tm, LANES), lambda i, j: (i, j, 0)), vec]
    args += [gn.reshape(1, d), sc, sh, w]
    specs += [pl.BlockSpec((1, d), lambda i, j: (0, 0)), vec, vec,
              pl.BlockSpec((d, ncols), lambda i, j: (0, 0))]
    if has_dft:
        args.append(wd)
        specs.append(pl.BlockSpec(wd.shape, lambda i, j: (0, 0)))
    out_shape = [jax.ShapeDtypeStruct((b, n, n_main), BF16)]
    out_specs = [pl.BlockSpec((1, tm, n_main), lambda i, j: (i, j, 0))]
    if has_dft:
        out_shape.append(jax.ShapeDtypeStruct((b, n, 2 * GROUP_WIDTH), BF16))
        out_specs.append(pl.BlockSpec((1, tm, 2 * GROUP_WIDTH), lambda i, j: (i, j, 0)))
    if has_resid:
        out_shape.append(jax.ShapeDtypeStruct((b, n, d), F32))
        out_specs.append(tok)
    return pl.pallas_call(
        functools.partial(_win_kernel, n_main=n_main, has_dft=has_dft, has_resid=has_resid),
        grid=(b, n // tm),
        in_specs=specs,
        out_specs=out_specs,
        out_shape=out_shape,
        compiler_params=_cparams(("arbitrary", "arbitrary")),
        name="norm_in_proj",
    )(*args)


def _convpool_kernel(av_ref, ab_ref, ac_ref, p_ref, avp_ref, acp_ref, pp_ref, avn_ref, acn_ref, pn_ref,
                     cw_ref, pw_ref, ps_ref, o_ref, *, rows, seq):
    j = pl.program_id(1)
    first = j == 0
    last = j == pl.num_programs(1) - 1
    ext_rows = rows + 2 * HALO

    def ext(main_ref, prev_ref, next_ref):
        prev = jnp.where(first, 0.0, prev_ref[0].astype(F32))
        nxt = jnp.where(last, 0.0, next_ref[0].astype(F32))
        return jnp.concatenate([prev, main_ref[0].astype(F32), nxt], axis=0)

    def shifted(a, s):
        return a if s % ext_rows == 0 else pltpu.roll(a, s % ext_rows, axis=0)

    def centre(a):
        return a[HALO:HALO + rows]

    z = ext(ac_ref, acp_ref, acn_ref) * ext(av_ref, avp_ref, avn_ref)
    cw = cw_ref[...]
    conv = cw[0:1] * shifted(z, 1) + cw[1:2] * z + cw[2:3] * shifted(z, -1)
    y_conv = ab_ref[0].astype(F32) * centre(conv)

    e = ext(p_ref, pp_ref, pn_ref)
    run = e
    sums = {}
    width = 1
    while width < max(POOL_WINDOWS):
        run = run + shifted(run, width)
        width *= 2
        sums[width] = run
    lane_grp = lax.broadcasted_iota(jnp.int32, (1, GROUP_WIDTH), 1) // POOL_GROUP
    t = j * rows + lax.broadcasted_iota(jnp.int32, (rows, 1), 0)
    win = None
    cnt = None
    for gi, w in enumerate(POOL_WINDOWS):
        ws = centre(shifted(sums[w], -(w // 2 - 1)))
        c = (jnp.minimum(t + w // 2, seq) - jnp.maximum(t - w // 2, 0)).astype(F32)
        if win is None:
            win, cnt = ws, jnp.broadcast_to(c, ws.shape)
        else:
            sel = lane_grp == gi
            win = jnp.where(sel, ws, win)
            cnt = jnp.where(sel, c, cnt)
    dlt = win / cnt - centre(e)
    y_pool = _dot(dlt.astype(BF16), pw_ref[...]) * ps_ref[...]
    o_ref[0] = jnp.concatenate([y_conv, y_pool], axis=1).astype(BF16)


def _conv_pool(u, conv_w, pool_bd, pool_scale, rows=512):
    b, n, _ = u.shape
    rows = min(rows, n)
    nh = n // HALO
    rb = rows // HALO
    gw = GROUP_WIDTH

    def main(col):
        return pl.BlockSpec((1, rows, gw), lambda i, j: (i, j, col))

    def prev(col):
        return pl.BlockSpec((1, HALO, gw), lambda i, j: (i, jnp.maximum(j * rb - 1, 0), col))

    def nxt(col):
        return pl.BlockSpec((1, HALO, gw), lambda i, j: (i, jnp.minimum((j + 1) * rb, nh - 1), col))

    small = lambda shape: pl.BlockSpec(shape, lambda i, j: (0, 0))
    return pl.pallas_call(
        functools.partial(_convpool_kernel, rows=rows, seq=n),
        grid=(b, n // rows),
        in_specs=[main(0), main(1), main(2), main(3),
                  prev(0), prev(2), prev(3), nxt(0), nxt(2), nxt(3),
                  small((CONV_K, gw)), small((gw, gw)), small((1, gw))],
        out_specs=pl.BlockSpec((1, rows, 2 * gw), lambda i, j: (i, j, 0)),
        out_shape=jax.ShapeDtypeStruct((b, n, 2 * gw), BF16),
        compiler_params=_cparams(("arbitrary", "arbitrary")),
        name="conv_pool",
    )(u, u, u, u, u, u, u, u, u, u, conv_w, pool_bd, pool_scale.reshape(1, gw))


def _na_geometry(grid_rows):
    n_groups = grid_rows // NA_ROWS
    band_rows = 4 * NA_BAND_BLOCKS
    kr = min(WIN_ROWS, grid_rows)
    starts, patterns = [], []
    for g in range(n_groups):
        s_blk = 2 * g - 1
        starts.append(s_blk)
        pat = np.full((NA_ROWS, band_rows), 2 * WIN_ROWS - 1, np.int32)
        for rl in range(NA_ROWS):
            r = g * NA_ROWS + rl
            r0 = int(np.clip(r - kr // 2, 0, grid_rows - kr))
            for jj in range(band_rows):
                a = 4 * s_blk + jj
                if r0 <= a < r0 + kr:
                    pat[rl, jj] = a - r + (WIN_ROWS - 1)
        masked = 2 * WIN_ROWS - 1
        assert (pat[:NA_ROWS // 2, band_rows - 4:] == masked).all() and (pat[NA_ROWS // 2:, :4] == masked).all()
        patterns.append(pat)
    pairs = np.stack(patterns).reshape(n_groups, NA_ROWS, band_rows // 2, 2)
    slab_pairs, slab_of = np.unique(pairs.reshape(-1, 2), axis=0, return_inverse=True)
    return starts, slab_pairs, slab_of.reshape(-1).astype(np.int32)


def _na_bias_slabs(rpb, slab_pairs):
    col = np.arange(GRID_W)
    c0 = np.clip(col - WIN_COLS // 2, 0, GRID_W - WIN_COLS)
    rel = col[None, :] - col[:, None] + (WIN_COLS - 1)
    valid = (col[None, :] >= c0[:, None]) & (col[None, :] < c0[:, None] + WIN_COLS)
    select = (rel[None] == np.arange(2 * WIN_COLS - 1)[:, None, None]).astype(np.float32)
    t1 = jnp.einsum('hrd,dqw->hrqw', rpb.astype(F32), jnp.asarray(select), precision=lax.Precision.HIGHEST)
    t1 = jnp.where(jnp.asarray(valid)[None, None], t1, NEG)
    t1 = jnp.concatenate([t1, jnp.full((N_HEADS, 1, GRID_W, GRID_W), NEG, F32)], axis=1)
    return jnp.concatenate([jnp.take(t1, jnp.asarray(slab_pairs[:, 0]), axis=1),
                            jnp.take(t1, jnp.asarray(slab_pairs[:, 1]), axis=1)], axis=-1)


def _attend(q, keys, vals, biases):
    qs = (q.astype(F32) * (HEAD_DIM ** -0.5)).astype(BF16)
    lane_head = lax.broadcasted_iota(jnp.int32, (1, GROUP_WIDTH), 1) // HEAD_DIM
    lane = lax.broadcasted_iota(jnp.int32, (1, GROUP_WIDTH), 1)
    out = jnp.zeros((q.shape[0], GROUP_WIDTH), F32)
    for h in range(N_HEADS):
        mh = lane_head == h
        qh = jnp.where(mh, qs, jnp.zeros_like(qs))
        s = []
        for kb, bias in zip(keys, biases):
            sb = _dot_nt(qh, kb)
            s.append(sb if bias is None else sb + bias(h))
        m = functools.reduce(jnp.maximum, [jnp.max(sb, axis=-1, keepdims=True) for sb in s])
        p = [jnp.exp((sb - m).astype(BF16)) for sb in s]
        sum_lane = ((h + 1) % N_HEADS) * HEAD_DIM
        o = functools.reduce(lambda a, c: a + c,
                             [_dot(pb, jnp.where(lane == sum_lane, jnp.ones_like(vb), vb)) for pb, vb in zip(p, vals)])
        out = jnp.where(mh, o * (1.0 / o[:, sum_lane:sum_lane + 1]), out)
    return out


def _na_kernel(start_ref, slab_ref, *refs):
    del start_ref
    q_ref = refs[0]
    k_refs = refs[1:1 + NA_BAND_BLOCKS]
    v_refs = refs[1 + NA_BAND_BLOCKS:1 + 2 * NA_BAND_BLOCKS]
    kc_ref, vc_ref, tbl_ref, o_ref = refs[1 + 2 * NA_BAND_BLOCKS:]
    keys = [r[0] for r in k_refs] + [kc_ref[0]]
    vals = [r[0] for r in v_refs] + [vc_ref[0]]
    slabs_per_row = 2 * NA_BAND_BLOCKS
    base = pl.program_id(0) * (NA_ROWS * slabs_per_row)
    half_rows = NA_ROWS // 2
    nq = half_rows * GRID_W

    def bias(h, blk, half):
        rows = []
        for rl in range(half * half_rows, (half + 1) * half_rows):
            idx = base + rl * slabs_per_row + 2 * blk
            rows.append(jnp.concatenate([tbl_ref[h, slab_ref[idx]], tbl_ref[h, slab_ref[idx + 1]]], axis=1))
        return jnp.concatenate(rows, axis=0)

    for half in range(2):
        blks = list(range(half, half + NA_BAND_BLOCKS - 1))
        o_ref[0, half * nq:(half + 1) * nq, :] = _attend(
            q_ref[0, half * nq:(half + 1) * nq, :],
            [keys[blk] for blk in blks] + [keys[-1]], [vals[blk] for blk in blks] + [vals[-1]],
            [functools.partial(bias, blk=blk, half=half) for blk in blks] + [None]).astype(BF16)


def _neighborhood_attention(u, uc, kc_col, rpb):
    b, n, _ = u.shape
    l = uc.shape[1]
    grid_rows = n // GRID_W
    starts, slab_pairs, slab_of = _na_geometry(grid_rows)
    tbl = _na_bias_slabs(rpb, slab_pairs)
    nq = NA_ROWS * GRID_W
    gw = GROUP_WIDTH

    def band(col, jj):
        return pl.BlockSpec((1, NA_KBLK, gw), lambda g, i, st, sl: (i, jnp.clip(st[g] + jj, 0, n // NA_KBLK - 1), col))

    return pl.pallas_call(
        _na_kernel,
        grid_spec=pltpu.PrefetchScalarGridSpec(
            num_scalar_prefetch=2,
            grid=(grid_rows // NA_ROWS, b),
            in_specs=[pl.BlockSpec((1, nq, gw), lambda g, i, st, sl: (i, g, 4))]
            + [band(5, jj) for jj in range(NA_BAND_BLOCKS)]
            + [band(6, jj) for jj in range(NA_BAND_BLOCKS)]
            + [pl.BlockSpec((1, l, gw), lambda g, i, st, sl: (i, 0, kc_col)),
               pl.BlockSpec((1, l, gw), lambda g, i, st, sl: (i, 0, kc_col + 1)),
               pl.BlockSpec(tbl.shape, lambda g, i, st, sl: (0, 0, 0, 0))],
            out_specs=pl.BlockSpec((1, nq, gw), lambda g, i, st, sl: (i, g, 0)),
        ),
        out_shape=jax.ShapeDtypeStruct((b, n, gw), BF16),
        compiler_params=_cparams(("arbitrary", "arbitrary")),
        name="neighborhood_attn",
    )(jnp.asarray(starts, jnp.int32), jnp.asarray(slab_of), u, *([u] * (2 * NA_BAND_BLOCKS)), uc, uc, tbl)


def _ctx_attn_kernel(q_ref, k_ref, v_ref, o_ref):
    o_ref[0] = _attend(q_ref[0], [k_ref[0]], [v_ref[0]], [None]).astype(BF16)


def _context_attention(uc):
    b, l, _ = uc.shape
    gw = GROUP_WIDTH
    return pl.pallas_call(
        _ctx_attn_kernel,
        grid=(b,),
        in_specs=[pl.BlockSpec((1, l, gw), lambda i, col=col: (i, 0, col)) for col in (4, 5, 6)],
        out_specs=pl.BlockSpec((1, l, gw), lambda i: (i, 0, 0)),
        out_shape=jax.ShapeDtypeStruct((b, l, gw), BF16),
        compiler_params=_cparams(("arbitrary",)),
        name="context_attn",
    )(uc, uc, uc)


def _channel_dft():
    j = np.arange(HEAD_DIM)
    ang = 2.0 * np.pi * ((j[:, None] * j[None, :]) % HEAD_DIM) / HEAD_DIM
    eye = np.eye(N_HEADS)
    c = np.kron(eye, np.cos(ang)) / math.sqrt(HEAD_DIM)
    s = np.kron(eye, np.sin(ang)) / math.sqrt(HEAD_DIM)
    return jnp.asarray(np.concatenate([c, s], axis=1), F32).astype(BF16)


def _position_dft(n):
    t = np.arange(n)
    ang = 2.0 * np.pi * ((t[:, None] * t[None, :]) % n) / n
    c = jnp.asarray(np.cos(ang) / math.sqrt(n), F32)
    s = jnp.asarray(-np.sin(ang) / math.sqrt(n), F32)
    return c.astype(BF16), s.astype(BF16)


def _fourier_positions(n):
    r = int(round(math.sqrt(n)))
    if r * r == n and r % 16 == 0 and n > 256:
        ma, mb = _split_dft_matrices(n)
        return lambda g: _position_dft_split(g, ma, mb)
    cmat, smat = _position_dft(n)
    return lambda g: _position_dft_apply(g, cmat, smat)


def _dft_kernel(c_ref, s_ref, g_ref, o_ref, acc_ref):
    k = pl.program_id(1)

    @pl.when(k == 0)
    def _():
        acc_ref[...] = jnp.zeros_like(acc_ref)

    c = c_ref[...]
    s = s_ref[...]
    for i in range(g_ref.shape[0]):
        g = g_ref[i]
        acc_ref[i] += _dot(c, g[:, :GROUP_WIDTH]) + _dot(s, g[:, GROUP_WIDTH:])

    @pl.when(k == pl.num_programs(1) - 1)
    def _():
        o_ref[...] = acc_ref[...].astype(BF16)


def _split_dft_matrices(n):
    r = int(round(math.sqrt(n)))
    i = np.arange(r)
    t = r * i[None, None, :] + i[:, None, None]
    ang = 2.0 * np.pi * ((i[None, :, None] * t) % n) / n
    cs, sn = np.cos(ang) / math.sqrt(r), np.sin(ang) / math.sqrt(r)
    ma = np.concatenate([np.concatenate([cs, -sn], axis=2), np.concatenate([-sn, -cs], axis=2)], axis=1)
    phi = 2.0 * np.pi * ((i[:, None] * i[None, :]) % r) / r
    mb = np.concatenate([np.cos(phi), np.sin(phi)], axis=1) / math.sqrt(r)
    return jnp.asarray(ma, F32).astype(BF16), jnp.asarray(mb, F32).astype(BF16)


def _split_dft_kernel(g_ref, ma_ref, mb_ref, o_ref, gs_ref, zr_ref, zi_ref, ys_ref):
    r = mb_ref.shape[0]
    half = GROUP_WIDTH // LANES
    for j in range(2 * half):
        gs_ref[j] = g_ref[0, :, j * LANES:(j + 1) * LANES].astype(F32)

    def every_rth(ref, slabs, start):
        return jnp.concatenate([ref[j, pl.ds(start, r, stride=r), :] for j in slabs], axis=1)

    def put_every_rth(ref, start, val):
        for j in range(half):
            ref[j, pl.ds(start, r, stride=r), :] = val[:, j * LANES:(j + 1) * LANES]

    for c in range(r):
        x = jnp.concatenate([every_rth(gs_ref, range(half), c), every_rth(gs_ref, range(half, 2 * half), c)], axis=0)
        bc = _dot(ma_ref[c], x.astype(BF16))
        put_every_rth(zr_ref, c, bc[:r])
        put_every_rth(zi_ref, c, bc[r:])
    for k1 in range(r):
        rows = slice(k1 * r, (k1 + 1) * r)
        z = jnp.concatenate([jnp.concatenate([zr_ref[j, rows, :] for j in range(half)], axis=1),
                             jnp.concatenate([zi_ref[j, rows, :] for j in range(half)], axis=1)], axis=0)
        put_every_rth(ys_ref, k1, _dot(mb_ref[...], z.astype(BF16)))
    for j in range(half):
        o_ref[0, :, j * LANES:(j + 1) * LANES] = ys_ref[j].astype(BF16)


def _position_dft_split(g, ma, mb):
    b, n, _ = g.shape
    r = mb.shape[0]
    gw = GROUP_WIDTH
    return pl.pallas_call(
        _split_dft_kernel,
        grid=(b,),
        in_specs=[pl.BlockSpec((1, n, 2 * gw), lambda i: (i, 0, 0)),
                  pl.BlockSpec((r, 2 * r, 2 * r), lambda i: (0, 0, 0)),
                  pl.BlockSpec((r, 2 * r), lambda i: (0, 0))],
        out_specs=pl.BlockSpec((1, n, gw), lambda i: (i, 0, 0)),
        out_shape=jax.ShapeDtypeStruct((b, n, gw), BF16),
        scratch_shapes=[pltpu.VMEM((2 * gw // LANES, n, LANES), F32), pltpu.VMEM((gw // LANES, n, LANES), F32),
                        pltpu.VMEM((gw // LANES, n, LANES), F32), pltpu.VMEM((gw // LANES, n, LANES), F32)],
        compiler_params=_cparams(("arbitrary",)),
        name="position_dft_split",
    )(g, ma, mb)


def _position_dft_apply(g, cmat, smat, tm=1024, tk=512):
    b, n, _ = g.shape
    tm, tk = min(tm, n), min(tk, n)
    gw = GROUP_WIDTH
    return pl.pallas_call(
        _dft_kernel,
        grid=(n // tm, n // tk),
        in_specs=[pl.BlockSpec((tm, tk), lambda i, k: (i, k)),
                  pl.BlockSpec((tm, tk), lambda i, k: (i, k)),
                  pl.BlockSpec((b, tk, 2 * gw), lambda i, k: (0, k, 0))],
        out_specs=pl.BlockSpec((b, tm, gw), lambda i, k: (0, i, 0)),
        out_shape=jax.ShapeDtypeStruct((b, n, gw), BF16),
        scratch_shapes=[pltpu.VMEM((b, tm, gw), F32)],
        compiler_params=_cparams(("arbitrary", "arbitrary")),
        name="position_dft",
    )(cmat, smat, g)


def _route(logits_t, rb_ref):
    m = logits_t.shape[1]
    score, biased = [], []
    for e in range(N_EXPERTS):
        le = logits_t[e:e + 1, :]
        se = 1.0 / (1.0 + jnp.exp(-le))
        score.append(se)
        biased.append(se + rb_ref[e:e + 1, :])
    npg = EXPERTS_PER_GROUP
    grp_score = []
    for g in range(N_EXPERT_GROUPS):
        bg = biased[g * npg:(g + 1) * npg]
        pairs = [bg[i] + bg[j] for i in range(npg) for j in range(i + 1, npg)]
        grp_score.append(functools.reduce(jnp.maximum, pairs))
    best = jnp.zeros((1, m), jnp.int32)
    cur = grp_score[0]
    for g in range(1, N_EXPERT_GROUPS):
        upd = grp_score[g] > cur
        best = jnp.where(upd, g, best)
        cur = jnp.where(upd, grp_score[g], cur)
    vb, vu = [], []
    for i in range(npg):
        b_i, u_i = biased[i], score[i]
        for g in range(1, N_EXPERT_GROUPS):
            sel = best == g
            b_i = jnp.where(sel, biased[g * npg + i], b_i)
            u_i = jnp.where(sel, score[g * npg + i], u_i)
        vb.append(b_i)
        vu.append(u_i)
    i1 = jnp.zeros((1, m), jnp.int32)
    c1, u1 = vb[0], vu[0]
    for i in range(1, npg):
        upd = vb[i] > c1
        i1 = jnp.where(upd, i, i1)
        c1 = jnp.where(upd, vb[i], c1)
        u1 = jnp.where(upd, vu[i], u1)
    i2 = jnp.full((1, m), -1, jnp.int32)
    c2 = jnp.full((1, m), -jnp.inf, F32)
    u2 = jnp.zeros((1, m), F32)
    for i in range(npg):
        upd = (i1 != i) & (vb[i] > c2)
        i2 = jnp.where(upd, i, i2)
        c2 = jnp.where(upd, vb[i], c2)
        u2 = jnp.where(upd, vu[i], u2)
    tot = u1 + u2
    w1, w2 = u1 / tot, u2 / tot
    lo = jnp.minimum(i1, i2)
    hi = jnp.maximum(i1, i2)
    pair = jnp.where(lo == 0, 0, jnp.where(lo == 1, 3, 5)) + (hi - lo - 1)
    first_is_lo = i1 < i2
    return best * PAIRS_PER_GROUP + pair, jnp.where(first_is_lo, w1, w2), jnp.where(first_is_lo, w2, w1)


def _three_way(v):
    hi = v.astype(BF16).astype(F32)
    mid = (v - hi).astype(BF16).astype(F32)
    return hi, mid, v - hi - mid


def _wout_kernel(ycp_ref, ya_ref, yf_ref, x_ref, g1_ref, w_ref, gn_ref, sc_ref, sh_ref,
                 rwa_ref, rwb_ref, rb_ref, tri_ref, ltri_ref,
                 x1_ref, rows_ref, meta_ref, cnt_ref):
    gw = GROUP_WIDTH
    tm = x_ref.shape[1]
    mix = (_dot(ycp_ref[0], w_ref[0:2 * gw, :]) + _dot(ya_ref[0], w_ref[2 * gw:3 * gw, :])
           + _dot(yf_ref[0], w_ref[3 * gw:4 * gw, :]))
    x1 = x_ref[0] + g1_ref[0] * mix
    x1_ref[0] = x1
    h2 = _modulated_norm(x1, gn_ref[...], sc_ref[0], sh_ref[0])
    hh, hl = _split(h2)
    la = _dot(hh, rwa_ref[...]).T
    lb = _dot(hl, rwb_ref[...]).T
    logits_t = la[0:N_EXPERTS] + la[N_EXPERTS:2 * N_EXPERTS] + lb[0:N_EXPERTS]
    cls_t, ga_t, gb_t = _route(logits_t, rb_ref)

    onehot = lax.broadcasted_iota(jnp.int32, (CLS_PAD, tm), 0) == cls_t
    onehot_b = jnp.where(onehot, 1.0, 0.0).astype(BF16)
    prefix = _dot(onehot_b, tri_ref[...])
    cnt = jnp.sum(onehot_b.astype(F32), axis=1, keepdims=True)
    padded = jnp.floor((cnt + (CHUNK - 1)) * (1.0 / CHUNK)) * CHUNK
    seg_start = _dot(ltri_ref[...], jnp.broadcast_to(padded, (CLS_PAD, LANES)).astype(BF16))[:, 0:1]
    local_pos = jnp.sum(jnp.where(onehot, prefix + seg_start, 0.0), axis=0, keepdims=True)
    perm = (lax.broadcasted_iota(jnp.int32, (rows_ref.shape[2], tm), 0) == local_pos.astype(jnp.int32))

    side = jnp.concatenate(list(_three_way(ga_t)) + list(_three_way(gb_t)) + [local_pos,
                           jnp.zeros((LANES - 7, tm), F32)], axis=0).T
    meta_ref[0] = side
    lane = lax.broadcasted_iota(jnp.int32, (1, LANES), 1)
    rows = jnp.concatenate([h2.astype(BF16), jnp.where(lane < 6, side, 0.0).astype(BF16)], axis=1)
    rows_ref[0, 0] = _dot(jnp.where(perm, 1.0, 0.0).astype(BF16), rows)
    cnt_ref[0, 0] = _dot_nt(jnp.ones((8, tm), BF16), onehot_b)


def _output_projection(ycp, ya, yf, x, g1, w_out, gn, sc, sh, rwa, rwb, rb, tm=512):
    b, n, d = x.shape
    tm = min(tm, n)
    nt = n // tm
    gw = GROUP_WIDTH
    tri = jnp.asarray(np.triu(np.ones((tm, tm), np.float32), 1), BF16)
    ltri = jnp.asarray(np.tril(np.ones((CLS_PAD, CLS_PAD), np.float32), -1), BF16)
    tokb = lambda width: pl.BlockSpec((1, tm, width), lambda i, j: (i, j, 0))
    vec = pl.BlockSpec((1, 1, d), lambda i, j: (i, 0, 0))
    full = lambda shape: pl.BlockSpec(shape, lambda i, j: (0, 0))
    return pl.pallas_call(
        _wout_kernel,
        grid=(b, nt),
        in_specs=[tokb(2 * gw), tokb(gw), tokb(gw), tokb(d), vec, full((4 * gw, d)), full((1, d)), vec, vec,
                  full((d, LANES)), full((d, LANES)), full((N_EXPERTS, 1)), full((tm, tm)), full((CLS_PAD, CLS_PAD))],
        out_specs=[tokb(d),
                   pl.BlockSpec((1, 1, _sorted_rows(tm), d + LANES), lambda i, j: (i, j, 0, 0)),
                   tokb(LANES),
                   pl.BlockSpec((1, 1, 8, CLS_PAD), lambda i, j: (i, j, 0, 0))],
        out_shape=[jax.ShapeDtypeStruct((b, n, d), F32),
                   jax.ShapeDtypeStruct((b, nt, _sorted_rows(tm), d + LANES), F32),
                   jax.ShapeDtypeStruct((b, n, LANES), F32),
                   jax.ShapeDtypeStruct((b, nt, 8, CLS_PAD), F32)],
        compiler_params=_cparams(("arbitrary", "arbitrary")),
        name="out_proj_router",
    )(ycp, ya, yf, x, g1, w_out, gn.reshape(1, d), sc, sh, rwa, rwb, rb, tri, ltri)


def _moe_layout(cnt, chunks_per_tile, n_slots):
    n_prod = cnt.shape[0]
    n_chunks = n_prod * chunks_per_tile
    ids = lambda n: jnp.arange(n, dtype=jnp.int32)
    nch = (cnt + CHUNK - 1) // CHUNK
    seg_src = ids(n_prod)[:, None] * chunks_per_tile + jnp.cumsum(nch, axis=1) - nch
    used = jnp.sum(nch, axis=1)
    cls_chunks = jnp.sum(nch, axis=0)
    cls_padded = (cls_chunks + TILE_CHUNKS - 1) // TILE_CHUNKS * TILE_CHUNKS
    cls_end = jnp.cumsum(cls_padded)
    cls_start = cls_end - cls_padded
    pad_before = jnp.cumsum(cls_padded - cls_chunks) - (cls_padded - cls_chunks)
    seg_dst = (cls_start[None, :] + jnp.cumsum(nch, axis=0) - nch).T.reshape(-1)
    seg_n = nch.T.reshape(-1)
    seg_src = seg_src.T.reshape(-1)
    slot = ids(n_slots)
    cls_of_slot = jnp.minimum(jnp.sum((cls_end[None, :] <= slot[:, None]).astype(jnp.int32), axis=1), N_CLASSES - 1)
    by_cls = cls_of_slot[:, None] == ids(N_CLASSES)[None, :]
    cls_lookup = lambda table: jnp.sum(jnp.where(by_cls, table[None, :], 0), axis=1)
    seg_of_slot = jnp.sum((seg_dst[None, :] <= slot[:, None]).astype(jnp.int32), axis=1) - 1
    by_seg = seg_of_slot[:, None] == ids(seg_n.shape[0])[None, :]
    seg_lookup = lambda table: jnp.sum(jnp.where(by_seg, table[None, :], 0), axis=1)
    q = slot - cls_lookup(cls_start)
    real = q < cls_lookup(cls_chunks)
    chunk = seg_lookup(seg_src) + slot - seg_lookup(seg_dst)
    src = jnp.where(real, chunk, 0)
    dst = jnp.where(real, chunk, n_chunks + jnp.minimum(cls_lookup(pad_before) + q - cls_lookup(cls_chunks),
                                                        DUMP_CHUNKS - 1))
    n_active = cls_end[-1:] // TILE_CHUNKS
    tile_cls = cls_of_slot[::TILE_CHUNKS]
    by_tile = tile_cls[:, None] == ids(N_CLASSES)[None, :]
    ea = jnp.sum(jnp.where(by_tile, jnp.asarray(CLASS_EA, jnp.int32)[None, :], 0), axis=1)
    eb = jnp.sum(jnp.where(by_tile, jnp.asarray(CLASS_EB, jnp.int32)[None, :], 0), axis=1)
    n_pad = jnp.sum(cls_padded - cls_chunks, keepdims=True)
    tail_start = jnp.concatenate([ids(n_prod) * chunks_per_tile + used, n_chunks + n_pad])
    tail_len = jnp.concatenate([chunks_per_tile - used, DUMP_CHUNKS - n_pad])
    return src, dst, ea, eb, n_active, tail_start, tail_len


def _moe_kernel(src_ref, dst_ref, ea_ref, eb_ref, n_active_ref, tail_start_ref, tail_len_ref,
                xs_ref, w1a_ref, w3a_ref, w2a_ref, w1b_ref, w3b_ref, w2b_ref, ys_ref,
                xbuf, ybuf, zbuf, gsem, ssem, zsem, *, n_prod):
    del ea_ref, eb_ref
    d = ybuf.shape[2]
    tile_rows = TILE_CHUNKS * CHUNK
    j = pl.program_id(0)
    n_active = n_active_ref[0]
    slot = j % 2

    def rows(ref, chunk):
        return ref.at[pl.ds(pl.multiple_of(chunk * CHUNK, CHUNK), CHUNK)]

    def start_gather(tile, buf):
        for k in range(TILE_CHUNKS):
            pltpu.make_async_copy(rows(xs_ref, src_ref[tile * TILE_CHUNKS + k]),
                                  xbuf.at[buf, pl.ds(k * CHUNK, CHUNK)], gsem.at[buf]).start()

    def start_scatter(tile, buf):
        for k in range(TILE_CHUNKS):
            pltpu.make_async_copy(ybuf.at[buf, pl.ds(k * CHUNK, CHUNK)],
                                  rows(ys_ref, dst_ref[tile * TILE_CHUNKS + k]), ssem.at[buf]).start()

    def wait_gather(buf):
        pltpu.make_async_copy(xs_ref.at[pl.ds(0, tile_rows)], xbuf.at[buf], gsem.at[buf]).wait()

    def wait_scatter(buf):
        pltpu.make_async_copy(ybuf.at[buf], ys_ref.at[pl.ds(0, tile_rows)], ssem.at[buf]).wait()

    def per_unwritten_chunk(wait):
        def range_body(t, carry):
            def chunk_body(i, c):
                copy = pltpu.make_async_copy(zbuf, rows(ys_ref, tail_start_ref[t] + i), zsem)
                copy.wait() if wait else copy.start()
                return c
            lax.fori_loop(0, tail_len_ref[t], chunk_body, 0)
            return carry
        lax.fori_loop(0, n_prod + 1, range_body, 0)

    @pl.when(j == 0)
    def _():
        zbuf[...] = jnp.zeros_like(zbuf)
        per_unwritten_chunk(wait=False)
        start_gather(0, 0)

    @pl.when(j < n_active)
    def _():
        @pl.when(j + 1 < n_active)
        def _():
            start_gather(j + 1, 1 - slot)

        wait_gather(slot)

        @pl.when(j >= 2)
        def _():
            wait_scatter(slot)

        row = xbuf[slot]
        h = row[:, :d].astype(BF16)

        def expert(w1_ref, w3_ref, w2_ref):
            a = _dot(h, w1_ref[0])
            he = (a / (1.0 + jnp.exp(-a))) * _dot(h, w3_ref[0])
            return _dot(he.astype(BF16), w2_ref[0])

        ga = row[:, d:d + 1] + row[:, d + 1:d + 2] + row[:, d + 2:d + 3]
        gb = row[:, d + 3:d + 4] + row[:, d + 4:d + 5] + row[:, d + 5:d + 6]
        ybuf[slot] = ga * expert(w1a_ref, w3a_ref, w2a_ref) + gb * expert(w1b_ref, w3b_ref, w2b_ref)
        start_scatter(j, slot)

    @pl.when(j == n_active - 1)
    def _():
        @pl.when(j >= 1)
        def _():
            wait_scatter(1 - slot)
        wait_scatter(slot)
        per_unwritten_chunk(wait=True)


def _moe(rows, counts, w1, w3, w2):
    b, nt, sorted_rows, width = rows.shape
    _, d, hid = w1.shape
    n_prod = b * nt
    n_chunks = n_prod * sorted_rows // CHUNK
    n_tiles = -(-n_chunks // TILE_CHUNKS) + N_CLASSES
    cnt = counts[:, :, 0, :N_CLASSES].reshape(n_prod, N_CLASSES).astype(jnp.int32)
    tables = _moe_layout(cnt, sorted_rows // CHUNK, n_tiles * TILE_CHUNKS)
    hbm = pl.BlockSpec(memory_space=pl.ANY)
    n_pre = len(tables)
    wa = lambda shape: pl.BlockSpec(shape, lambda i, *pre: (pre[2][i], 0, 0))
    wb = lambda shape: pl.BlockSpec(shape, lambda i, *pre: (pre[3][i], 0, 0))
    tile_rows = TILE_CHUNKS * CHUNK
    return pl.pallas_call(
        functools.partial(_moe_kernel, n_prod=n_prod),
        grid_spec=pltpu.PrefetchScalarGridSpec(
            num_scalar_prefetch=n_pre,
            grid=(n_tiles,),
            in_specs=[hbm, wa((1, d, hid)), wa((1, d, hid)), wa((1, hid, d)),
                      wb((1, d, hid)), wb((1, d, hid)), wb((1, hid, d))],
            out_specs=hbm,
            scratch_shapes=[pltpu.VMEM((2, tile_rows, width), F32), pltpu.VMEM((2, tile_rows, d), F32),
                            pltpu.VMEM((CHUNK, d), F32), pltpu.SemaphoreType.DMA((2,)),
                            pltpu.SemaphoreType.DMA((2,)), pltpu.SemaphoreType.DMA],
        ),
        out_shape=jax.ShapeDtypeStruct(((n_chunks + DUMP_CHUNKS) * CHUNK, d), F32),
        compiler_params=_cparams(("arbitrary",)),
        name="moe_grouped",
    )(*tables, rows.reshape(n_chunks * CHUNK, width), w1, w3, w2, w1, w3, w2)


def _final_kernel(x_ref, ys_ref, meta_ref, g2_ref, gn_ref, o_ref):
    x = x_ref[0] + g2_ref[0] * _unsort(ys_ref, meta_ref)
    ms = jnp.mean(x * x, axis=-1, keepdims=True)
    o_ref[0] = x * lax.rsqrt(ms + EPS) * gn_ref[...]


def _final_norm(x1, resid, gn, tm=512):
    b, n, d = x1.shape
    ys, meta, g2 = resid
    tm = min(tm, n)
    tok = pl.BlockSpec((1, tm, d), lambda i, j: (i, j, 0))
    return pl.pallas_call(
        _final_kernel,
        grid=(b, n // tm),
        in_specs=[tok, pl.BlockSpec((_sorted_rows(tm), d), lambda i, j: (i * (n // tm) + j, 0)),
                  pl.BlockSpec((1, tm, LANES), lambda i, j: (i, j, 0)),
                  pl.BlockSpec((1, 1, d), lambda i, j: (i, 0, 0)), pl.BlockSpec((1, d), lambda i, j: (0, 0))],
        out_specs=tok,
        out_shape=jax.ShapeDtypeStruct((b, n, d), F32),
        compiler_params=_cparams(("arbitrary", "arbitrary")),
        name="final_norm",
    )(x1, ys, meta, g2, gn.reshape(1, d))


def _block_diag(blocks):
    g, c, _ = blocks.shape
    out = jnp.zeros((g * c, g * c), blocks.dtype)
    for i in range(g):
        out = lax.dynamic_update_slice(out, blocks[i], (i * c, i * c))
    return out


def kernel(x, c, ctx, c_ctx, ada_w, ada_b, norm1_g, norm2_g, w_in, conv_w, pool_w, pool_scale, na_rpb, w_out,
           router_w, router_b, moe_w1, moe_w3, moe_w2, final_g):
    b, n, d = x.shape
    l = ctx.shape[1]
    depth = ada_w.shape[0]
    gw = GROUP_WIDTH

    rows = -(-(b + 1) // 8) * 8
    c_all = jnp.concatenate([c, c_ctx[None, :], jnp.zeros((rows - b - 1, d), F32)], axis=0)
    mod = _modulation(c_all, ada_w, ada_b)

    def lat_mod(layer, i):
        return mod[layer, :b, i * d:(i + 1) * d].reshape(b, 1, d)

    def ctx_mod(layer, i):
        return jnp.broadcast_to(mod[layer, b, i * d:(i + 1) * d].reshape(1, 1, d), (b, 1, d))

    wd = _channel_dft()
    fourier_lat = _fourier_positions(n)
    fourier_ctx = _fourier_positions(l)

    rw_hi, rw_lo = _split(router_w.astype(F32))
    zpad = lambda k: jnp.zeros((d, LANES - k * N_EXPERTS), BF16)
    rwa = jnp.concatenate([rw_hi, rw_lo, zpad(2)], axis=1)
    rwb = jnp.concatenate([rw_hi, zpad(1)], axis=1)
    rb = router_b.astype(F32).reshape(N_EXPERTS, 1)

    xc = ctx
    resid = None
    resid_c = None
    for layer in range(depth):
        last = layer == depth - 1
        w_in_l = w_in[layer].astype(BF16)
        w_out_l = w_out[layer].astype(BF16)
        w1 = moe_w1[layer].astype(BF16)
        w3 = moe_w3[layer].astype(BF16)
        w2 = moe_w2[layer].astype(BF16)
        pool_bd = _block_diag(pool_w[layer].astype(BF16))

        outs = _input_projection(x, norm1_g[layer], lat_mod(layer, 1), lat_mod(layer, 0), w_in_l, wd=wd, resid=resid)
        u, g = outs[0], outs[1]
        if resid is not None:
            x = outs[2]
        if last:
            outs_c = _input_projection(xc, norm1_g[layer], ctx_mod(layer, 1), ctx_mod(layer, 0),
                                       w_in_l[:, 5 * gw:7 * gw], resid=resid_c)
            uc, kc_col = outs_c[0], 0
        else:
            outs_c = _input_projection(xc, norm1_g[layer], ctx_mod(layer, 1), ctx_mod(layer, 0), w_in_l, wd=wd,
                                       resid=resid_c)
            uc, gc, kc_col = outs_c[0], outs_c[1], 5
            if resid_c is not None:
                xc = outs_c[2]

        ycp = _conv_pool(u, conv_w[layer], pool_bd, pool_scale[layer])
        ya = _neighborhood_attention(u, uc, kc_col, na_rpb[layer])
        yf = fourier_lat(g)
        x, rows_l, meta_l, counts_l = _output_projection(
            ycp, ya, yf, x, lat_mod(layer, 2), w_out_l, norm2_g[layer], lat_mod(layer, 4), lat_mod(layer, 3),
            rwa, rwb, rb)
        resid = (_moe(rows_l, counts_l, w1, w3, w2), meta_l, lat_mod(layer, 5))
        if not last:
            ycp_c = _conv_pool(uc, conv_w[layer], pool_bd, pool_scale[layer])
            ya_c = _context_attention(uc)
            yf_c = fourier_ctx(gc)
            xc, rows_c, meta_c, counts_c = _output_projection(
                ycp_c, ya_c, yf_c, xc, ctx_mod(layer, 2), w_out_l, norm2_g[layer], ctx_mod(layer, 4),
                ctx_mod(layer, 3), rwa, rwb, rb)
            resid_c = (_moe(rows_c, counts_c, w1, w3, w2), meta_c, ctx_mod(layer, 5))
    return _final_norm(x, resid, final_g)
```
